```python
import jax, jax.numpy as jnp
from jax import lax
import numpy as np

D_MODEL = 2048
BATCH = 8
SEQ = 4096
DEPTH = 4
DEC_BATCH = 2
DEC_SEQ = 4096
PAST_LEN = 128

D_CONV = D_MODEL // 2
N_CONV_GROUPS = 8
CONV_A_WIDTH = 3
D_RNN = D_MODEL
N_RNN_HEADS = 8
RNN_BLOCK = D_RNN // N_RNN_HEADS
CONV_B_WIDTH = 4
CONV_B_PAD_LEFT = 2
LRU_C = 8.0
D_FF = ((8 * D_MODEL // 3 + 127) // 128) * 128
N_IN = 3 * D_CONV + 2 * D_RNN + 2 * D_MODEL
SPLITS = (D_CONV, 2 * D_CONV, 3 * D_CONV, 3 * D_CONV + D_RNN,
          3 * D_CONV + 2 * D_RNN, 3 * D_CONV + 2 * D_RNN + D_MODEL)
ALPHA = (2.0 * DEPTH) ** 0.25
BETA = (8.0 * DEPTH) ** -0.25
LN_EPS = 1e-5

kernel_name = "hybrid_conv_rglru_macaron_deepnorm_encoder"


def layer_norm(x, g, b):
    xf = x.astype(jnp.float32)
    mu = jnp.mean(xf, axis=-1, keepdims=True)
    var = jnp.mean(jnp.square(xf - mu), axis=-1, keepdims=True)
    y = (xf - mu) * lax.rsqrt(var + LN_EPS)
    return (y * g.astype(jnp.float32) + b.astype(jnp.float32)).astype(x.dtype)


def swiglu_ffn(x, w_gate_up, w_down):
    gate, up = jnp.split(jnp.einsum('bsd,df->bsf', x, w_gate_up), 2, axis=-1)
    return jnp.einsum('bsf,fd->bsd', jax.nn.silu(gate) * up, w_down)


def depthwise_conv(x, w, pad_left):
    k, c = w.shape
    return lax.conv_general_dilated(
        x, w[:, None, :].astype(x.dtype), window_strides=(1,),
        padding=[(pad_left, k - 1 - pad_left)],
        dimension_numbers=('NWC', 'WIO', 'NWC'), feature_group_count=c)


def _linear_combine(left, right):
    a_l, b_l = left
    a_r, b_r = right
    return a_l * a_r, a_r * b_l + b_r


def rglru_direction(xc, w_gate, b_gate, lam, reverse):
    bsz, s, _ = xc.shape
    xh = xc.reshape(bsz, s, N_RNN_HEADS, RNN_BLOCK)
    gates = jnp.einsum('bshi,ghij->gbshj', xh, w_gate).reshape(2, bsz, s, D_RNN)
    gates = gates.astype(jnp.float32) + b_gate.astype(jnp.float32)[:, None, None, :]
    r = jax.nn.sigmoid(gates[0])
    i = jax.nn.sigmoid(gates[1])
    log_a = -LRU_C * r * jax.nn.softplus(-lam.astype(jnp.float32))
    a = jnp.exp(log_a)
    u = jnp.sqrt(-jnp.expm1(2.0 * log_a)) * (i * xc.astype(jnp.float32))
    _, h = lax.associative_scan(_linear_combine, (a, u), reverse=reverse, axis=1)
    return h


def parallel_mixer(x, w_in, b_in, conv_a_w, w_out_a, conv_b_w, conv_b_b,
                   lru_w_gate, lru_b_gate, lru_lambda, w_out_b, w_o, b_o):
    proj = jnp.einsum('bsd,dn->bsn', x, w_in) + b_in
    c_gate, b_gate, v, x_rnn, y_rnn, g_a, g_b = jnp.split(proj, SPLITS, axis=-1)
    y_a = b_gate * depthwise_conv(c_gate * v, conv_a_w, CONV_A_WIDTH // 2)
    y_a = jnp.einsum('bsc,cd->bsd', y_a, w_out_a)
    xc = depthwise_conv(x_rnn, conv_b_w, CONV_B_PAD_LEFT) + conv_b_b
    h = (rglru_direction(xc, lru_w_gate[0], lru_b_gate[0], lru_lambda[0], False)
         + rglru_direction(xc, lru_w_gate[1], lru_b_gate[1], lru_lambda[1], True))
    y_b = h.astype(x.dtype) * jax.nn.gelu(y_rnn, approximate=True)
    y_b = jnp.einsum('bsc,cd->bsd', y_b, w_out_b)
    merged = jax.nn.sigmoid(g_a) * y_a + jax.nn.sigmoid(g_b) * y_b
    return jnp.einsum('bsd,de->bse', merged, w_o) + b_o


def trunk(x, ln_gain, ln_bias, ffn_w_gate_up, ffn_w_down, w_in, b_in, conv_a_w, w_out_a,
          conv_b_w, conv_b_b, lru_w_gate, lru_b_gate, lru_lambda, w_out_b, w_o, b_o):
    for l in range(DEPTH):
        x = layer_norm(ALPHA * x + 0.5 * swiglu_ffn(x, ffn_w_gate_up[l, 0], ffn_w_down[l, 0]),
                       ln_gain[l, 0], ln_bias[l, 0])
        x = layer_norm(ALPHA * x + parallel_mixer(x, w_in[l], b_in[l], conv_a_w[l], w_out_a[l],
                                                  conv_b_w[l], conv_b_b[l], lru_w_gate[l],
                                                  lru_b_gate[l], lru_lambda[l], w_out_b[l],
                                                  w_o[l], b_o[l]),
                       ln_gain[l, 1], ln_bias[l, 1])
        x = layer_norm(ALPHA * x + 0.5 * swiglu_ffn(x, ffn_w_gate_up[l, 1], ffn_w_down[l, 1]),
                       ln_gain[l, 2], ln_bias[l, 2])
    return x


def setup_inputs(seed: int = 0) -> dict:
    key = jax.random.key(seed)
    ks = jax.random.split(key, 20)
    nrm = jax.random.normal
    f32 = jnp.float32
    x_prompt = nrm(ks[0], (BATCH, SEQ, D_MODEL), f32)
    x_sample = nrm(ks[1], (DEC_BATCH, DEC_SEQ, D_MODEL), f32)
    ln_gain = 1.0 + 0.01 * nrm(ks[2], (DEPTH, 3, D_MODEL), f32)
    ln_bias = 0.01 * nrm(ks[3], (DEPTH, 3, D_MODEL), f32)
    ffn_w_gate_up = nrm(ks[4], (DEPTH, 2, D_MODEL, 2 * D_FF), f32) * D_MODEL ** -0.5
    ffn_w_down = nrm(ks[5], (DEPTH, 2, D_FF, D_MODEL), f32) * (BETA * D_FF ** -0.5)
    w_in = nrm(ks[6], (DEPTH, D_MODEL, N_IN), f32) * D_MODEL ** -0.5
    b_in = 0.01 * nrm(ks[7], (DEPTH, N_IN), f32)
    conv_a_w = nrm(ks[8], (DEPTH, CONV_A_WIDTH, D_CONV), f32) * CONV_A_WIDTH ** -0.5
    w_out_a = nrm(ks[9], (DEPTH, D_CONV, D_MODEL), f32) * (BETA * D_CONV ** -0.5)
    conv_b_w = nrm(ks[10], (DEPTH, CONV_B_WIDTH, D_RNN), f32) * CONV_B_WIDTH ** -0.5
    conv_b_b = 0.01 * nrm(ks[11], (DEPTH, D_RNN), f32)
    lru_w_gate = nrm(ks[12], (DEPTH, 2, 2, N_RNN_HEADS, RNN_BLOCK, RNN_BLOCK), f32) * RNN_BLOCK ** -0.5
    lru_b_gate = 0.01 * nrm(ks[13], (DEPTH, 2, 2, D_RNN), f32)
    a_c = jax.random.uniform(ks[14], (DEPTH, 2, D_RNN), f32, minval=0.9, maxval=0.999)
    a0 = a_c ** (1.0 / LRU_C)
    lru_lambda = jnp.log(a0) - jnp.log1p(-a0)
    w_out_b = nrm(ks[15], (DEPTH, D_RNN, D_MODEL), f32) * (BETA * D_RNN ** -0.5)
    w_o = nrm(ks[16], (DEPTH, D_MODEL, D_MODEL), f32) * (BETA * D_MODEL ** -0.5)
    b_o = 0.01 * nrm(ks[17], (DEPTH, D_MODEL), f32)
    return {"x_prompt": x_prompt, "x_sample": x_sample, "ln_gain": ln_gain, "ln_bias": ln_bias,
            "ffn_w_gate_up": ffn_w_gate_up, "ffn_w_down": ffn_w_down, "w_in": w_in, "b_in": b_in,
            "conv_a_w": conv_a_w, "w_out_a": w_out_a, "conv_b_w": conv_b_w, "conv_b_b": conv_b_b,
            "lru_w_gate": lru_w_gate, "lru_b_gate": lru_b_gate, "lru_lambda": lru_lambda,
            "w_out_b": w_out_b, "w_o": w_o, "b_o": b_o}


def reference(x_prompt, x_sample, ln_gain, ln_bias, ffn_w_gate_up, ffn_w_down, w_in, b_in,
              conv_a_w, w_out_a, conv_b_w, conv_b_b, lru_w_gate, lru_b_gate, lru_lambda,
              w_out_b, w_o, b_o):
    y_prompt = trunk(x_prompt, ln_gain, ln_bias, ffn_w_gate_up, ffn_w_down, w_in, b_in,
                     conv_a_w, w_out_a, conv_b_w, conv_b_b, lru_w_gate, lru_b_gate,
                     lru_lambda, w_out_b, w_o, b_o)
    y_sample = trunk(x_sample, ln_gain, ln_bias, ffn_w_gate_up, ffn_w_down, w_in, b_in,
                     conv_a_w, w_out_a, conv_b_w, conv_b_b, lru_w_gate, lru_b_gate,
                     lru_lambda, w_out_b, w_o, b_o)
    return (y_prompt, y_sample)
```

```python
import functools

import jax
import jax.numpy as jnp
from jax import lax
from jax.experimental import pallas as pl
from jax.experimental.pallas import tpu as pltpu

LN_EPS = 1e-5
LRU_C = 8.0
CONV_B_PAD_LEFT = 2
SUBLANES = 8
V7X_VMEM_LIMIT_BYTES = 60000 * 1024

BF16 = jnp.bfloat16
F32 = jnp.float32


def _round_up(x, m):
    return (x + m - 1) // m * m


def _layer_norm(y, g, b):
    mu = jnp.mean(y, axis=-1, keepdims=True)
    yc = y - mu
    var = jnp.mean(yc * yc, axis=-1, keepdims=True)
    return yc * lax.rsqrt(var + LN_EPS) * g + b


def _compiler_params(semantics):
    return pltpu.CompilerParams(dimension_semantics=semantics,
                                vmem_limit_bytes=V7X_VMEM_LIMIT_BYTES)


def _ffn_kernel(s_ref, x_ref, wg_ref, wu_ref, wd_ref, g_ref, b_ref, o_ref, xb_ref, *, alpha):
    del s_ref
    j = pl.program_id(1)

    @pl.when(j == 0)
    def _():
        xb_ref[...] = x_ref[...].astype(BF16)
        o_ref[...] = jnp.zeros_like(o_ref)

    xb = xb_ref[...]
    gate = jnp.dot(xb, wg_ref[...], preferred_element_type=F32)
    up = jnp.dot(xb, wu_ref[...], preferred_element_type=F32)
    hidden = (gate * jax.nn.sigmoid(gate)) * up
    o_ref[...] += jnp.dot(hidden.astype(BF16), wd_ref[...], preferred_element_type=F32)

    @pl.when(j == pl.num_programs(1) - 1)
    def _():
        y = alpha * x_ref[...] + 0.5 * o_ref[...]
        o_ref[...] = _layer_norm(y, g_ref[...], b_ref[...])


def _ffn_call(x, sel, w_gu, w_d, ln_g, ln_b, *, alpha, tm, fc):
    t, d = x.shape
    fp = w_d.shape[2]
    nf = fp // fc
    grid_spec = pltpu.PrefetchScalarGridSpec(
        num_scalar_prefetch=1,
        grid=(t // tm, nf),
        in_specs=[
            pl.BlockSpec((tm, d), lambda i, j, s: (i, 0)),
            pl.BlockSpec((None, None, d, fc), lambda i, j, s: (s[0], s[1], 0, j)),
            pl.BlockSpec((None, None, d, fc), lambda i, j, s: (s[0], s[1], 0, nf + j)),
            pl.BlockSpec((None, None, fc, d), lambda i, j, s: (s[0], s[1], j, 0)),
            pl.BlockSpec((None, None, 1, d), lambda i, j, s: (s[0], s[2], 0, 0)),
            pl.BlockSpec((None, None, 1, d), lambda i, j, s: (s[0], s[2], 0, 0)),
        ],
        out_specs=pl.BlockSpec((tm, d), lambda i, j, s: (i, 0)),
        scratch_shapes=[pltpu.VMEM((tm, d), BF16)],
    )
    return pl.pallas_call(
        functools.partial(_ffn_kernel, alpha=alpha),
        grid_spec=grid_spec,
        out_shape=jax.ShapeDtypeStruct((t, d), F32),
        compiler_params=_compiler_params(("parallel", "arbitrary")),
        name="ffn",
    )(sel, x, w_gu, w_gu, w_d, ln_g, ln_b)


def _inproj_kernel(s_ref, x_ref, w_ref, b_ref, o_ref, xb_ref):
    del s_ref

    @pl.when(pl.program_id(1) == 0)
    def _():
        xb_ref[...] = x_ref[...].astype(BF16)

    o_ref[...] = jnp.dot(xb_ref[...], w_ref[...], preferred_element_type=F32) + b_ref[...]


def _inproj_call(x, sel, w_in, b_in, *, tm, tn):
    t, d = x.shape
    n = w_in.shape[2]
    grid_spec = pltpu.PrefetchScalarGridSpec(
        num_scalar_prefetch=1,
        grid=(t // tm, n // tn),
        in_specs=[
            pl.BlockSpec((tm, d), lambda i, j, s: (i, 0)),
            pl.BlockSpec((None, d, tn), lambda i, j, s: (s[0], 0, j)),
            pl.BlockSpec((None, 1, tn), lambda i, j, s: (s[0], 0, j)),
        ],
        out_specs=pl.BlockSpec((tm, tn), lambda i, j, s: (i, j)),
        scratch_shapes=[pltpu.VMEM((tm, d), BF16)],
    )
    return pl.pallas_call(
        _inproj_kernel,
        grid_spec=grid_spec,
        out_shape=jax.ShapeDtypeStruct((t, n), F32),
        compiler_params=_compiler_params(("parallel", "arbitrary")),
        name="inproj",
    )(sel, x, w_in, b_in)


def _shift_prev(x, k, halo_rows):
    tc = x.shape[0]
    rows = lax.broadcasted_iota(jnp.int32, x.shape, 0)
    out = pltpu.roll(x, k, 0)
    for r in range(k):
        out = jnp.where(rows == r, halo_rows[r], out)
    del tc
    return out


def _shift_next(x, halo_row):
    tc = x.shape[0]
    rows = lax.broadcasted_iota(jnp.int32, x.shape, 0)
    out = pltpu.roll(x, tc - 1, 0)
    return jnp.where(rows == tc - 1, halo_row, out)


def _conv_a_kernel(s_ref, c_ref, b_ref, v_ref, cp_ref, vp_ref, cn_ref, vn_ref, w_ref, o_ref):
    del s_ref
    i = pl.program_id(1)
    has_prev = (i > 0).astype(F32)
    has_next = (i < pl.num_programs(1) - 1).astype(F32)
    cv = c_ref[...] * v_ref[...]
    last = SUBLANES - 1
    cv_prev = cp_ref[last:last + 1, :] * vp_ref[last:last + 1, :] * has_prev
    cv_next = cn_ref[0:1, :] * vn_ref[0:1, :] * has_next
    w = w_ref[...]
    conv = (w[0:1, :] * _shift_prev(cv, 1, [cv_prev]) + w[1:2, :] * cv
            + w[2:3, :] * _shift_next(cv, cv_next))
    o_ref[...] = (b_ref[...] * conv).astype(o_ref.dtype)


def _conv_a_call(proj, sel, conv_a_w, *, tc):
    nb, s_len, _ = proj.shape
    dc = conv_a_w.shape[2]
    ni = s_len // tc
    hb = tc // SUBLANES
    nhb = s_len // SUBLANES

    def main(col):
        return pl.BlockSpec((None, tc, dc), lambda n, i, s: (n, i, col))

    def prev(col):
        return pl.BlockSpec((None, SUBLANES, dc),
                            lambda n, i, s: (n, jnp.maximum(i * hb - 1, 0), col))

    def nxt(col):
        return pl.BlockSpec((None, SUBLANES, dc),
                            lambda n, i, s: (n, jnp.minimum((i + 1) * hb, nhb - 1), col))

    grid_spec = pltpu.PrefetchScalarGridSpec(
        num_scalar_prefetch=1,
        grid=(nb, ni),
        in_specs=[main(0), main(1), main(2), prev(0), prev(2), nxt(0), nxt(2),
                  pl.BlockSpec((None, 3, dc), lambda n, i, s: (s[0], 0, 0))],
        out_specs=pl.BlockSpec((None, tc, dc), lambda n, i, s: (n, i, 0)),
    )
    return pl.pallas_call(
        _conv_a_kernel,
        grid_spec=grid_spec,
        out_shape=jax.ShapeDtypeStruct((nb, s_len, dc), BF16),
        compiler_params=_compiler_params(("parallel", "arbitrary")),
        name="conv_a",
    )(sel, proj, proj, proj, proj, proj, proj, proj, conv_a_w)


def _lru_kernel(s_ref, x_ref, xp_ref, xn_ref, cw_ref, cb_ref, wg_ref, bg_ref, lam_ref, *rest,
                reverse, rnn_block):
    del s_ref
    if reverse:
        y_ref, hf_ref, o_ref, a_s, u_s, carry = rest
    else:
        o_ref, a_s, u_s, carry = rest
    step = pl.program_id(2)
    ni = pl.num_programs(2)
    chunk = ni - 1 - step if reverse else step
    tc, w = x_ref.shape
    ngroups = tc // SUBLANES

    @pl.when(step == 0)
    def _():
        carry[...] = jnp.zeros_like(carry)

    has_prev = (chunk > 0).astype(F32)
    has_next = (chunk < ni - 1).astype(F32)
    x = x_ref[...]
    p2 = xp_ref[SUBLANES - 2:SUBLANES - 1, :] * has_prev
    p1 = xp_ref[SUBLANES - 1:SUBLANES, :] * has_prev
    n1 = xn_ref[0:1, :] * has_next
    cw = cw_ref[...]
    xc = (cw[0:1, :] * _shift_prev(x, 2, [p2, p1]) + cw[1:2, :] * _shift_prev(x, 1, [p1])
          + cw[2:3, :] * x + cw[3:4, :] * _shift_next(x, n1)) + cb_ref[...]

    xcb = xc.astype(BF16)
    r_parts, i_parts = [], []
    for h in range(w // rnn_block):
        g = jnp.dot(xcb[:, h * rnn_block:(h + 1) * rnn_block], wg_ref[h],
                    preferred_element_type=F32)
        r_parts.append(g[:, :rnn_block])
        i_parts.append(g[:, rnn_block:])
    bg = bg_ref[...]
    r = jax.nn.sigmoid(jnp.concatenate(r_parts, axis=1) + bg[0:1, :])
    ig = jax.nn.sigmoid(jnp.concatenate(i_parts, axis=1) + bg[1:2, :])
    neg_lam = -lam_ref[...]
    softplus = jnp.maximum(neg_lam, 0.0) + jnp.log1p(jnp.exp(-jnp.abs(neg_lam)))
    log_a = (-LRU_C) * r * softplus
    a = jnp.exp(log_a)
    u = jnp.sqrt(-jnp.tanh(log_a) * (a * a + 1.0)) * (ig * xc)

    rin = lax.broadcasted_iota(jnp.int32, (tc, w), 0) % SUBLANES
    for sh in (1, 2, 4):
        if reverse:
            a_o = pltpu.roll(a, tc - sh, 0)
            u_o = pltpu.roll(u, tc - sh, 0)
            m = rin < SUBLANES - sh
        else:
            a_o = pltpu.roll(a, sh, 0)
            u_o = pltpu.roll(u, sh, 0)
            m = rin >= sh
        u = jnp.where(m, a * u_o + u, u)
        a = jnp.where(m, a * a_o, a)
    a_s[...] = a
    u_s[...] = u

    def group(g, h):
        gi = ngroups - 1 - g if reverse else g
        r0 = pl.multiple_of(gi * SUBLANES, SUBLANES)
        hh = u_s[pl.ds(r0, SUBLANES), :] + a_s[pl.ds(r0, SUBLANES), :] * h
        u_s[pl.ds(r0, SUBLANES), :] = hh
        return hh[0:1, :] if reverse else hh[SUBLANES - 1:SUBLANES, :]

    carry[...] = lax.fori_loop(0, ngroups, group, carry[...])

    if reverse:
        h_tot = hf_ref[...] + u_s[...]
        o_ref[...] = (h_tot * jax.nn.gelu(y_ref[...], approximate=True)).astype(o_ref.dtype)
    else:
        o_ref[...] = u_s[...]


def _lru_call(proj, sel, conv_b_w, conv_b_b, w_gate, b_gate, lam, h_fwd, *, reverse, tc, w):
    nb, s_len, _ = proj.shape
    dr = conv_b_w.shape[2]
    rb = w_gate.shape[3]
    x_col0 = 3
    y_col0 = 3 + dr // w
    ncb = dr // w
    ni = s_len // tc
    hb = tc // SUBLANES
    nhb = s_len // SUBLANES
    direction = 1 if reverse else 0

    def ch(i):
        return ni - 1 - i if reverse else i

    in_specs = [
        pl.BlockSpec((None, tc, w), lambda n, c, i, s: (n, ch(i), x_col0 + c)),
        pl.BlockSpec((None, SUBLANES, w),
                     lambda n, c, i, s: (n, jnp.maximum(ch(i) * hb - 1, 0), x_col0 + c)),
        pl.BlockSpec((None, SUBLANES, w),
                     lambda n, c, i, s: (n, jnp.minimum((ch(i) + 1) * hb, nhb - 1), x_col0 + c)),
        pl.BlockSpec((None, 4, w), lambda n, c, i, s: (s[0], 0, c)),
        pl.BlockSpec((None, 1, w), lambda n, c, i, s: (s[0], 0, c)),
        pl.BlockSpec((None, None, w // rb, rb, 2 * rb), lambda n, c, i, s: (s[0], direction, c, 0, 0)),
        pl.BlockSpec((None, None, 2, w), lambda n, c, i, s: (s[0], direction, 0, c)),
        pl.BlockSpec((None, None, 1, w), lambda n, c, i, s: (s[0], direction, 0, c)),
    ]
    args = [sel, proj, proj, proj, conv_b_w, conv_b_b, w_gate, b_gate, lam]
    if reverse:
        in_specs += [
            pl.BlockSpec((None, tc, w), lambda n, c, i, s: (n, ch(i), y_col0 + c)),
            pl.BlockSpec((None, tc, w), lambda n, c, i, s: (n, ch(i), c)),
        ]
        args += [proj, h_fwd]
    grid_spec = pltpu.PrefetchScalarGridSpec(
        num_scalar_prefetch=1,
        grid=(nb, ncb, ni),
        in_specs=in_specs,
        out_specs=pl.BlockSpec((None, tc, w), lambda n, c, i, s: (n, ch(i), c)),
        scratch_shapes=[pltpu.VMEM((tc, w), F32), pltpu.VMEM((tc, w), F32),
                        pltpu.VMEM((1, w), F32)],
    )
    return pl.pallas_call(
        functools.partial(_lru_kernel, reverse=reverse, rnn_block=rb),
        grid_spec=grid_spec,
        out_shape=jax.ShapeDtypeStruct((nb, s_len, dr), BF16 if reverse else F32),
        compiler_params=_compiler_params(("parallel", "parallel", "arbitrary")),
        name="lru_bwd" if reverse else "lru_fwd",
    )(*args)


def _mixer_out_kernel(s_ref, x_ref, ya_ref, yb_ref, *rest, alpha, ncol):
    del s_ref
    ga_refs = rest[:ncol]
    gb_refs = rest[ncol:2 * ncol]
    woa_ref, wob_ref, wo_ref, bo_ref, g_ref, b_ref, o_ref = rest[2 * ncol:]
    y_a = jnp.dot(ya_ref[...], woa_ref[...], preferred_element_type=F32)
    y_b = jnp.dot(yb_ref[...], wob_ref[...], preferred_element_type=F32)
    w = ga_refs[0].shape[1]
    parts = []
    for c in range(ncol):
        sl = slice(c * w, (c + 1) * w)
        parts.append(jax.nn.sigmoid(ga_refs[c][...]) * y_a[:, sl]
                     + jax.nn.sigmoid(gb_refs[c][...]) * y_b[:, sl])
    merged = jnp.concatenate(parts, axis=1).astype(BF16)
    out = jnp.dot(merged, wo_ref[...], preferred_element_type=F32) + bo_ref[...]
    o_ref[...] = _layer_norm(alpha * x_ref[...] + out, g_ref[...], b_ref[...])


def _mixer_out_call(x, sel, ya_pre, yb_pre, proj, w_out_a, w_out_b, w_o, b_o, ln_g, ln_b, *,
                    alpha, tm):
    t, d = x.shape
    dc = ya_pre.shape[1]
    dr = yb_pre.shape[1]
    w = dc
    ncol = d // w
    ga_col0 = 3 + 2 * (dr // w)
    gb_col0 = ga_col0 + ncol
    resident = pl.Buffered(1)

    def gate_spec(col):
        return pl.BlockSpec((tm, w), lambda i, s: (i, col))

    in_specs = (
        [pl.BlockSpec((tm, d), lambda i, s: (i, 0)),
         pl.BlockSpec((tm, dc), lambda i, s: (i, 0)),
         pl.BlockSpec((tm, dr), lambda i, s: (i, 0))]
        + [gate_spec(ga_col0 + c) for c in range(ncol)]
        + [gate_spec(gb_col0 + c) for c in range(ncol)]
        + [pl.BlockSpec((None, dc, d), lambda i, s: (s[0], 0, 0), pipeline_mode=resident),
           pl.BlockSpec((None, dr, d), lambda i, s: (s[0], 0, 0), pipeline_mode=resident),
           pl.BlockSpec((None, d, d), lambda i, s: (s[0], 0, 0), pipeline_mode=resident),
           pl.BlockSpec((None, 1, d), lambda i, s: (s[0], 0, 0)),
           pl.BlockSpec((None, None, 1, d), lambda i, s: (s[0], s[2], 0, 0)),
           pl.BlockSpec((None, None, 1, d), lambda i, s: (s[0], s[2], 0, 0))])
    grid_spec = pltpu.PrefetchScalarGridSpec(
        num_scalar_prefetch=1,
        grid=(t // tm,),
        in_specs=in_specs,
        out_specs=pl.BlockSpec((tm, d), lambda i, s: (i, 0)),
    )
    return pl.pallas_call(
        functools.partial(_mixer_out_kernel, alpha=alpha, ncol=ncol),
        grid_spec=grid_spec,
        out_shape=jax.ShapeDtypeStruct((t, d), F32),
        compiler_params=_compiler_params(("parallel",)),
        name="mixer_out",
    )(sel, x, ya_pre, yb_pre, *([proj] * (2 * ncol)), w_out_a, w_out_b, w_o, b_o, ln_g, ln_b)


def _pick_tile(n, target):
    if n <= target:
        return n
    for cand in range(target, 7, -1):
        if n % cand == 0 and cand % SUBLANES == 0:
            return cand
    return n


def _trunk(x_prompt, x_sample, ln_gain, ln_bias, ffn_w_gate_up, ffn_w_down, w_in, b_in,
           conv_a_w, w_out_a, conv_b_w, conv_b_b, lru_w_gate, lru_b_gate, lru_lambda,
           w_out_b, w_o, b_o, *, tiles=None):
    tiles = dict(tiles or {})
    depth = ln_gain.shape[0]
    alpha = (2.0 * depth) ** 0.25
    bp, s_len, d = x_prompt.shape
    bs, s_len2, _ = x_sample.shape
    assert s_len == s_len2, "prompt and sample sequences are processed as one batch of sequences"
    nb = bp + bs
    t = nb * s_len
    dc = conv_a_w.shape[2]
    dr = conv_b_w.shape[2]
    heads, rb = lru_w_gate.shape[3], lru_w_gate.shape[4]
    f = ffn_w_down.shape[2]
    n_in = w_in.shape[2]
    assert n_in == 3 * dc + 2 * dr + 2 * d and dr % dc == 0 and d % dc == 0 and dc % rb == 0
    assert heads * rb == dr and s_len % SUBLANES == 0

    fc = tiles.get("fc", 256)
    fp = _round_up(f, fc)
    tm_ffn = tiles.get("tm_ffn", _pick_tile(t, 1024))
    tm_in = tiles.get("tm_in", _pick_tile(t, 1024))
    tm_out = tiles.get("tm_out", _pick_tile(t, 256))
    tc = tiles.get("tc", _pick_tile(s_len, 512))

    gate_w, up_w = ffn_w_gate_up[..., :f], ffn_w_gate_up[..., f:]
    pad_cols = [(0, 0)] * 3 + [(0, fp - f)]
    w_gu = jnp.concatenate([jnp.pad(gate_w, pad_cols), jnp.pad(up_w, pad_cols)],
                           axis=-1).astype(BF16)
    w_d = jnp.pad(ffn_w_down, [(0, 0), (0, 0), (0, fp - f), (0, 0)]).astype(BF16)
    w_in_b = w_in.astype(BF16)
    w_gate = jnp.transpose(lru_w_gate, (0, 1, 3, 4, 2, 5)).reshape(
        depth, 2, heads, rb, 2 * rb).astype(BF16)
    w_out_a_b = w_out_a.astype(BF16)
    w_out_b_b = w_out_b.astype(BF16)
    w_o_b = w_o.astype(BF16)
    ln_g = ln_gain[:, :, None, :]
    ln_b = ln_bias[:, :, None, :]
    b_in3 = b_in[:, None, :]
    conv_b_b3 = conv_b_b[:, None, :]
    lam4 = lru_lambda[:, :, None, :]
    b_o3 = b_o[:, None, :]

    x0 = jnp.concatenate([x_prompt, x_sample], axis=0).reshape(t, d)

    def layer(l, x):
        def sel(k, ln):
            return jnp.stack([l, jnp.int32(k), jnp.int32(ln)]).astype(jnp.int32)

        x = _ffn_call(x, sel(0, 0), w_gu, w_d, ln_g, ln_b, alpha=alpha, tm=tm_ffn, fc=fc)
        proj = _inproj_call(x, sel(0, 0), w_in_b, b_in3, tm=tm_in, tn=dc)
        proj3 = proj.reshape(nb, s_len, n_in)
        ya_pre = _conv_a_call(proj3, sel(0, 0), conv_a_w, tc=tc)
        h_fwd = _lru_call(proj3, sel(0, 0), conv_b_w, conv_b_b3, w_gate, lru_b_gate, lam4, None,
                          reverse=False, tc=tc, w=dc)
        yb_pre = _lru_call(proj3, sel(0, 0), conv_b_w, conv_b_b3, w_gate, lru_b_gate, lam4, h_fwd,
                           reverse=True, tc=tc, w=dc)
        x = _mixer_out_call(x, sel(0, 1), ya_pre.reshape(t, dc), yb_pre.reshape(t, dr), proj,
                            w_out_a_b, w_out_b_b, w_o_b, b_o3, ln_g, ln_b, alpha=alpha, tm=tm_out)
        x = _ffn_call(x, sel(1, 2), w_gu, w_d, ln_g, ln_b, alpha=alpha, tm=tm_ffn, fc=fc)
        return x

    x = lax.fori_loop(0, depth, layer, x0)
    x = x.reshape(nb, s_len, d)
    return x[:bp], x[bp:]


def kernel(x_prompt, x_sample, ln_gain, ln_bias, ffn_w_gate_up, ffn_w_down, w_in, b_in, conv_a_w, w_out_a, conv_b_w, conv_b_b, lru_w_gate, lru_b_gate, lru_lambda, w_out_b, w_o, b_o):
    return _trunk(x_prompt, x_sample, ln_gain, ln_bias, ffn_w_gate_up, ffn_w_down, w_in, b_in,
                  conv_a_w, w_out_a, conv_b_w, conv_b_b, lru_w_gate, lru_b_gate, lru_lambda,
                  w_out_b, w_o, b_o)
```

```python
import functools

import jax
import jax.numpy as jnp
from jax import lax
from jax.experimental import pallas as pl
from jax.experimental.pallas import tpu as pltpu

LN_EPS = 1e-5
LRU_C = 8.0
SUBLANES = 8
LOG2_E = 1.4426950408889634
GELU_C0 = 0.7978845608028654
GELU_C1 = GELU_C0 * 0.044715
V7X_VMEM_LIMIT_BYTES = 60000 * 1024

BF16 = jnp.bfloat16
F32 = jnp.float32


def _round_up(x, m):
    return (x + m - 1) // m * m


def _dot(a, b):
    return jnp.dot(a, b, preferred_element_type=F32)


def _layer_norm(y, g, b):
    mu = jnp.mean(y, axis=-1, keepdims=True)
    yc = y - mu
    var = jnp.mean(yc * yc, axis=-1, keepdims=True)
    return yc * lax.rsqrt(var + LN_EPS) * g + b


def _sigmoid(x):
    return 0.5 * jnp.tanh(0.5 * x) + 0.5


def _compiler_params(semantics):
    return pltpu.CompilerParams(dimension_semantics=semantics,
                                vmem_limit_bytes=V7X_VMEM_LIMIT_BYTES)


def _ffn_kernel(s_ref, x_ref, wg_ref, wu_ref, wd_ref, g_ref, b_ref, o_ref, xb_ref, *,
                alpha, row_block):
    del s_ref
    j = pl.program_id(1)
    last = pl.num_programs(1) - 1
    tm = x_ref.shape[0]

    def down(xb):
        gate = _dot(xb, wg_ref[...])
        up = _dot(xb, wu_ref[...])
        hidden = (gate * jax.nn.sigmoid(gate)) * up
        return _dot(hidden.astype(BF16), wd_ref[...])

    @pl.when(j == 0)
    def _():
        for r in range(0, tm, row_block):
            rows = pl.ds(r, row_block)
            xb = x_ref[rows, :].astype(BF16)
            xb_ref[rows, :] = xb
            o_ref[rows, :] = down(xb)

    @pl.when(jnp.logical_and(j > 0, j < last))
    def _():
        o_ref[...] += down(xb_ref[...])

    @pl.when(j == last)
    def _():
        for r in range(0, tm, row_block):
            rows = pl.ds(r, row_block)
            acc = o_ref[rows, :] + down(xb_ref[rows, :])
            y = alpha * x_ref[rows, :] + 0.5 * acc
            o_ref[rows, :] = _layer_norm(y, g_ref[...], b_ref[...])


def _ffn_call(x, sel, w_gu, w_d, ln_g, ln_b, *, alpha, tm, fc):
    t, d = x.shape
    fp = w_d.shape[2]
    nf = fp // fc
    assert nf >= 2
    grid_spec = pltpu.PrefetchScalarGridSpec(
        num_scalar_prefetch=1,
        grid=(t // tm, nf),
        in_specs=[
            pl.BlockSpec((tm, d), lambda i, j, s: (i, 0)),
            pl.BlockSpec((None, None, d, fc), lambda i, j, s: (s[0], s[1], 0, j)),
            pl.BlockSpec((None, None, d, fc), lambda i, j, s: (s[0], s[1], 0, nf + j)),
            pl.BlockSpec((None, None, fc, d), lambda i, j, s: (s[0], s[1], j, 0)),
            pl.BlockSpec((None, None, 1, d), lambda i, j, s: (s[0], s[2], 0, 0)),
            pl.BlockSpec((None, None, 1, d), lambda i, j, s: (s[0], s[2], 0, 0)),
        ],
        out_specs=pl.BlockSpec((tm, d), lambda i, j, s: (i, 0)),
        scratch_shapes=[pltpu.VMEM((tm, d), BF16)],
    )
    return pl.pallas_call(
        functools.partial(_ffn_kernel, alpha=alpha, row_block=min(tm, 256)),
        grid_spec=grid_spec,
        out_shape=jax.ShapeDtypeStruct((t, d), F32),
        compiler_params=_compiler_params(("parallel", "arbitrary")),
        name="ffn",
    )(sel, x, w_gu, w_gu, w_d, ln_g, ln_b)


def _inproj_kernel(s_ref, x_ref, w_ref, b_ref, o_ref, xb_ref):
    del s_ref

    @pl.when(pl.program_id(1) == 0)
    def _():
        xb_ref[...] = x_ref[...].astype(BF16)

    o_ref[...] = _dot(xb_ref[...], w_ref[...]) + b_ref[...]


def _inproj_call(x, sel, w_in, b_in, *, tm, tn):
    t, d = x.shape
    n = w_in.shape[2]
    grid_spec = pltpu.PrefetchScalarGridSpec(
        num_scalar_prefetch=1,
        grid=(t // tm, n // tn),
        in_specs=[
            pl.BlockSpec((tm, d), lambda i, j, s: (i, 0)),
            pl.BlockSpec((None, d, tn), lambda i, j, s: (s[0], 0, j)),
            pl.BlockSpec((None, 1, tn), lambda i, j, s: (s[0], 0, j)),
        ],
        out_specs=pl.BlockSpec((tm, tn), lambda i, j, s: (i, j)),
        scratch_shapes=[pltpu.VMEM((tm, d), BF16)],
    )
    return pl.pallas_call(
        _inproj_kernel,
        grid_spec=grid_spec,
        out_shape=jax.ShapeDtypeStruct((t, n), F32),
        compiler_params=_compiler_params(("parallel", "arbitrary")),
        name="inproj",
    )(sel, x, w_in, b_in)


def _seq_start_tile(tile_of_last_steps):
    sub = lax.broadcasted_iota(jnp.int32, tile_of_last_steps.shape, 0)
    return jnp.where(sub == 0, 0.0, pltpu.roll(tile_of_last_steps, 1, 0))


def _seq_end_tile(tile_of_first_steps):
    sub = lax.broadcasted_iota(jnp.int32, tile_of_first_steps.shape, 0)
    return jnp.where(sub == SUBLANES - 1, 0.0, pltpu.roll(tile_of_first_steps, SUBLANES - 1, 0))


def _tile_at(load_tile, r, is_edge, edge_tile):
    return jnp.where(is_edge, edge_tile, load_tile(r))


def _conv_a_kernel(s_ref, c_ref, b_ref, v_ref, w_ref, o_ref, *, rows_per_block):
    del s_ref
    s_len = c_ref.shape[0]
    nr = rows_per_block
    nblk = s_len // nr
    w = w_ref[...]
    w0, w1, w2 = w[0:1, :], w[1:2, :], w[2:3, :]

    def cv_tile(r):
        rows = pl.ds(pl.multiple_of(r, SUBLANES), SUBLANES)
        return c_ref[rows, :] * v_ref[rows, :]

    before_first = _seq_start_tile(cv_tile(s_len - SUBLANES))
    after_last = _seq_end_tile(cv_tile(0))

    def block(k, carry):
        r0 = pl.multiple_of(k * nr, SUBLANES)
        rows = pl.ds(r0, nr)
        cv = c_ref[rows, :] * v_ref[rows, :]
        prev_tile = _tile_at(cv_tile, jnp.maximum(r0 - SUBLANES, 0), k == 0, before_first)
        next_tile = _tile_at(cv_tile, jnp.minimum(r0 + nr, s_len - SUBLANES), k == nblk - 1,
                             after_last)
        cv_m1 = jnp.concatenate([prev_tile, cv[:nr - SUBLANES]], axis=0)
        cv_p1 = jnp.concatenate([cv[SUBLANES:], next_tile], axis=0)
        o_ref[rows, :] = (b_ref[rows, :] * (w0 * cv_m1 + w1 * cv + w2 * cv_p1)).astype(o_ref.dtype)
        return carry

    lax.fori_loop(0, nblk, block, 0)


def _conv_a_call(proj, sel, conv_a_w, *, wa, rows_per_block):
    nb, s_len, _ = proj.shape
    dc = conv_a_w.shape[2]
    ncb = dc // wa

    def col(base):
        return pl.BlockSpec((None, s_len, wa), lambda n, c, s: (n, 0, base * ncb + c))

    grid_spec = pltpu.PrefetchScalarGridSpec(
        num_scalar_prefetch=1,
        grid=(nb, ncb),
        in_specs=[col(0), col(1), col(2),
                  pl.BlockSpec((None, 3, wa), lambda n, c, s: (s[0], 0, c))],
        out_specs=pl.BlockSpec((None, s_len, wa), lambda n, c, s: (n, 0, c)),
    )
    return pl.pallas_call(
        functools.partial(_conv_a_kernel, rows_per_block=rows_per_block),
        grid_spec=grid_spec,
        out_shape=jax.ShapeDtypeStruct((nb, s_len, dc), BF16),
        compiler_params=_compiler_params(("parallel", "parallel")),
        name="conv_a",
    )(sel, proj, proj, proj, conv_a_w)


def _segment_carries(p_end, h_end, reverse):
    sub = lax.broadcasted_iota(jnp.int32, p_end.shape, 0)
    a, u = p_end, h_end
    for sh in (1, 2, 4):
        if reverse:
            a_o = pltpu.roll(a, SUBLANES - sh, 0)
            u_o = pltpu.roll(u, SUBLANES - sh, 0)
            m = sub < SUBLANES - sh
        else:
            a_o = pltpu.roll(a, sh, 0)
            u_o = pltpu.roll(u, sh, 0)
            m = sub >= sh
        u = jnp.where(m, a * u_o + u, u)
        a = jnp.where(m, a * a_o, a)
    return _seq_end_tile(u) if reverse else _seq_start_tile(u)


def _lru_kernel(s_ref, x_ref, y_ref, cw_ref, cb_ref, wg_ref, bg_ref, lam_ref, o_ref,
                pf_s, hf_s, pb_s, hb_s, *, rows_per_block, nsplit):
    del s_ref
    s_len, w = x_ref.shape
    nr = rows_per_block
    nblk = s_len // nr
    tiles_per_split = s_len // SUBLANES // nsplit
    blocks_per_split = nblk // nsplit

    def tile_rows(t):
        return pl.ds(pl.multiple_of(t * SUBLANES, SUBLANES), SUBLANES)

    def x_tile(r):
        return x_ref[pl.ds(pl.multiple_of(r, SUBLANES), SUBLANES), :]

    before_first_1 = _seq_start_tile(x_tile(s_len - SUBLANES))
    before_first_2 = _seq_start_tile(x_tile(s_len - 2 * SUBLANES))
    after_last = _seq_end_tile(x_tile(0))
    cw = cw_ref[...]
    cw0, cw1, cw2, cw3 = cw[0:1, :], cw[1:2, :], cw[2:3, :], cw[3:4, :]
    cb = cb_ref[...]
    bg = bg_ref[...]
    neg_lam = -lam_ref[...]
    log_a_scale = (-LRU_C) * (jnp.maximum(neg_lam, 0.0) + jnp.log1p(jnp.exp(-jnp.abs(neg_lam))))
    half_neg_scale = -0.5 * log_a_scale
    half_scale_log2 = (0.5 * LOG2_E) * log_a_scale

    def gates_block(k, carry):
        r0 = pl.multiple_of(k * nr, SUBLANES)
        rows = pl.ds(r0, nr)
        x = x_ref[rows, :]
        m1 = _tile_at(x_tile, jnp.maximum(r0 - SUBLANES, 0), k == 0, before_first_1)
        m2 = _tile_at(x_tile, jnp.maximum(r0 - 2 * SUBLANES, 0), k == 0, before_first_2)
        p1 = _tile_at(x_tile, jnp.minimum(r0 + nr, s_len - SUBLANES), k == nblk - 1, after_last)
        x_m1 = jnp.concatenate([m1, x[:nr - SUBLANES]], axis=0)
        x_m2 = jnp.concatenate([m2, m1, x[:nr - 2 * SUBLANES]], axis=0)
        x_p1 = jnp.concatenate([x[SUBLANES:], p1], axis=0)
        xc = cw0 * x_m2 + cw1 * x_m1 + cw2 * x + cw3 * x_p1 + cb
        half_xc = 0.5 * xc
        g = _dot(xc.astype(BF16), wg_ref[...])
        for d, (p_s, h_s) in enumerate(((pf_s, hf_s), (pb_s, hb_s))):
            t_r = jnp.tanh(g[:, (2 * d) * w:(2 * d + 1) * w] + bg[2 * d:2 * d + 1, :])
            t_i = jnp.tanh(g[:, (2 * d + 1) * w:(2 * d + 2) * w] + bg[2 * d + 1:2 * d + 2, :])
            c_neg = half_neg_scale[d:d + 1, :]
            c_log2 = half_scale_log2[d:d + 1, :]
            t_r1 = t_r + 1.0
            a = jnp.exp2(t_r1 * c_log2)
            one_minus_a2 = jnp.tanh(t_r1 * c_neg) * (a * a + 1.0)
            root = jnp.where(one_minus_a2 > 0.0, one_minus_a2 * lax.rsqrt(one_minus_a2), 0.0)
            p_s[rows, :] = a
            h_s[rows, :] = root * (t_i * half_xc + half_xc)
        return carry

    lax.fori_loop(0, nblk, gates_block, 0, unroll=2)

    def scan_tile(j, carry):
        out = []
        for q in range(nsplit):
            hf, pf, hb, pb = carry[q]
            rf = tile_rows(q * tiles_per_split + j)
            rb = tile_rows((q + 1) * tiles_per_split - 1 - j)
            a_f = pf_s[rf, :]
            hf = a_f * hf + hf_s[rf, :]
            pf = a_f * pf
            hf_s[rf, :] = hf
            pf_s[rf, :] = pf
            a_b = pb_s[rb, :]
            hb = a_b * hb + hb_s[rb, :]
            pb = a_b * pb
            hb_s[rb, :] = hb
            pb_s[rb, :] = pb
            out.append((hf, pf, hb, pb))
        return tuple(out)

    zeros = jnp.zeros((SUBLANES, w), F32)
    ones = jnp.ones((SUBLANES, w), F32)
    lax.fori_loop(0, tiles_per_split, scan_tile, ((zeros, ones, zeros, ones),) * nsplit, unroll=4)

    f_end = [tile_rows((q + 1) * tiles_per_split - 1) for q in range(nsplit)]
    b_end = [tile_rows(q * tiles_per_split) for q in range(nsplit)]
    p_f = [pf_s[r, :] for r in f_end]
    h_f = [hf_s[r, :] for r in f_end]
    p_b = [pb_s[r, :] for r in b_end]
    h_b = [hb_s[r, :] for r in b_end]
    p_seg, h_seg = p_f[0], h_f[0]
    for q in range(1, nsplit):
        h_seg = h_f[q] + p_f[q] * h_seg
        p_seg = p_f[q] * p_seg
    carry_f = [_segment_carries(p_seg, h_seg, reverse=False)]
    for q in range(nsplit - 1):
        carry_f.append(h_f[q] + p_f[q] * carry_f[q])
    p_seg, h_seg = p_b[nsplit - 1], h_b[nsplit - 1]
    for q in range(nsplit - 2, -1, -1):
        h_seg = h_b[q] + p_b[q] * h_seg
        p_seg = p_b[q] * p_seg
    carry_b = [None] * nsplit
    carry_b[nsplit - 1] = _segment_carries(p_seg, h_seg, reverse=True)
    for q in range(nsplit - 1, 0, -1):
        carry_b[q - 1] = h_b[q] + p_b[q] * carry_b[q]

    for q in range(nsplit):
        c_f = jnp.concatenate([carry_f[q]] * (nr // SUBLANES), axis=0)
        c_b = jnp.concatenate([carry_b[q]] * (nr // SUBLANES), axis=0)

        def out_block(k, carry, c_f=c_f, c_b=c_b):
            rows = pl.ds(pl.multiple_of(k * nr, SUBLANES), nr)
            h = (hf_s[rows, :] + pf_s[rows, :] * c_f) + (hb_s[rows, :] + pb_s[rows, :] * c_b)
            y = y_ref[rows, :]
            half_y = 0.5 * y
            t = jnp.tanh(y * (GELU_C0 + GELU_C1 * (y * y)))
            o_ref[rows, :] = (h * (half_y * t + half_y)).astype(o_ref.dtype)
            return carry

        lax.fori_loop(q * blocks_per_split, (q + 1) * blocks_per_split, out_block, 0)


def _lru_call(proj, sel, conv_b_w, conv_b_b, half_w_gate, half_b_gate, lam, *, wcol,
              rows_per_block):
    nb, s_len, _ = proj.shape
    dr = conv_b_w.shape[2]
    heads, rb = half_w_gate.shape[1], half_w_gate.shape[2]
    assert (3 * wcol) % rb == 0 and 3 * SUBLANES <= rows_per_block <= s_len
    nblk = s_len // rows_per_block
    nsplit = max(q for q in (1, 2, 4) if nblk % q == 0)
    x_col0 = 3 * wcol // rb
    y_col0 = x_col0 + heads
    grid_spec = pltpu.PrefetchScalarGridSpec(
        num_scalar_prefetch=1,
        grid=(nb, heads),
        in_specs=[
            pl.BlockSpec((None, s_len, rb), lambda n, h, s: (n, 0, x_col0 + h)),
            pl.BlockSpec((None, s_len, rb), lambda n, h, s: (n, 0, y_col0 + h)),
            pl.BlockSpec((None, 4, rb), lambda n, h, s: (s[0], 0, h)),
            pl.BlockSpec((None, 1, rb), lambda n, h, s: (s[0], 0, h)),
            pl.BlockSpec((None, None, rb, 4 * rb), lambda n, h, s: (s[0], h, 0, 0)),
            pl.BlockSpec((None, 4, rb), lambda n, h, s: (s[0], 0, h)),
            pl.BlockSpec((None, 2, rb), lambda n, h, s: (s[0], 0, h)),
        ],
        out_specs=pl.BlockSpec((None, s_len, rb), lambda n, h, s: (n, 0, h)),
        scratch_shapes=[pltpu.VMEM((s_len, rb), F32)] * 4,
    )
    return pl.pallas_call(
        functools.partial(_lru_kernel, rows_per_block=rows_per_block, nsplit=nsplit),
        grid_spec=grid_spec,
        out_shape=jax.ShapeDtypeStruct((nb, s_len, dr), BF16),
        compiler_params=_compiler_params(("parallel", "parallel")),
        name="lru",
    )(sel, proj, proj, conv_b_w, conv_b_b, half_w_gate, half_b_gate, lam)


def _mixer_out_kernel(s_ref, x_ref, ya_ref, yb_ref, *rest, alpha, ncol):
    del s_ref
    ga_refs = rest[:ncol]
    gb_refs = rest[ncol:2 * ncol]
    woa_ref, wob_ref, wo_ref, bo_ref, g_ref, b_ref, o_ref = rest[2 * ncol:]
    y_a = _dot(ya_ref[...], woa_ref[...])
    y_b = _dot(yb_ref[...], wob_ref[...])
    w = ga_refs[0].shape[1]
    parts = []
    for c in range(ncol):
        sl = slice(c * w, (c + 1) * w)
        parts.append(jax.nn.sigmoid(ga_refs[c][...]) * y_a[:, sl]
                     + jax.nn.sigmoid(gb_refs[c][...]) * y_b[:, sl])
    merged = jnp.concatenate(parts, axis=1).astype(BF16)
    out = _dot(merged, wo_ref[...]) + bo_ref[...]
    o_ref[...] = _layer_norm(alpha * x_ref[...] + out, g_ref[...], b_ref[...])


def _mixer_out_call(x, sel, ya_pre, yb_pre, proj, w_out_a, w_out_b, w_o, b_o, ln_g, ln_b, *,
                    alpha, tm):
    t, d = x.shape
    dc = ya_pre.shape[1]
    dr = yb_pre.shape[1]
    w = dc
    ncol = d // w
    ga_col0 = 3 + 2 * (dr // w)
    gb_col0 = ga_col0 + ncol
    resident = pl.Buffered(1)

    def gate_spec(col):
        return pl.BlockSpec((tm, w), lambda i, s: (i, col))

    in_specs = (
        [pl.BlockSpec((tm, d), lambda i, s: (i, 0)),
         pl.BlockSpec((tm, dc), lambda i, s: (i, 0)),
         pl.BlockSpec((tm, dr), lambda i, s: (i, 0))]
        + [gate_spec(ga_col0 + c) for c in range(ncol)]
        + [gate_spec(gb_col0 + c) for c in range(ncol)]
        + [pl.BlockSpec((None, dc, d), lambda i, s: (s[0], 0, 0), pipeline_mode=resident),
           pl.BlockSpec((None, dr, d), lambda i, s: (s[0], 0, 0), pipeline_mode=resident),
           pl.BlockSpec((None, d, d), lambda i, s: (s[0], 0, 0), pipeline_mode=resident),
           pl.BlockSpec((None, 1, d), lambda i, s: (s[0], 0, 0)),
           pl.BlockSpec((None, None, 1, d), lambda i, s: (s[0], s[2], 0, 0)),
           pl.BlockSpec((None, None, 1, d), lambda i, s: (s[0], s[2], 0, 0))])
    grid_spec = pltpu.PrefetchScalarGridSpec(
        num_scalar_prefetch=1,
        grid=(t // tm,),
        in_specs=in_specs,
        out_specs=pl.BlockSpec((tm, d), lambda i, s: (i, 0)),
    )
    return pl.pallas_call(
        functools.partial(_mixer_out_kernel, alpha=alpha, ncol=ncol),
        grid_spec=grid_spec,
        out_shape=jax.ShapeDtypeStruct((t, d), F32),
        compiler_params=_compiler_params(("parallel",)),
        name="mixer_out",
    )(sel, x, ya_pre, yb_pre, *([proj] * (2 * ncol)), w_out_a, w_out_b, w_o, b_o, ln_g, ln_b)


def _pick_tile(n, target):
    if n <= target:
        return n
    for cand in range(target, 7, -1):
        if n % cand == 0 and cand % SUBLANES == 0:
            return cand
    return n


def _interleave_segments(x):
    nb, s_len, d = x.shape
    return x.reshape(nb, SUBLANES, s_len // SUBLANES, d).transpose(0, 2, 1, 3).reshape(nb, s_len, d)


def _deinterleave_segments(x):
    nb, s_len, d = x.shape
    return x.reshape(nb, s_len // SUBLANES, SUBLANES, d).transpose(0, 2, 1, 3).reshape(nb, s_len, d)


def _trunk(x_prompt, x_sample, ln_gain, ln_bias, ffn_w_gate_up, ffn_w_down, w_in, b_in,
           conv_a_w, w_out_a, conv_b_w, conv_b_b, lru_w_gate, lru_b_gate, lru_lambda,
           w_out_b, w_o, b_o, *, tiles=None):
    tiles = dict(tiles or {})
    depth = ln_gain.shape[0]
    alpha = (2.0 * depth) ** 0.25
    bp, s_len, d = x_prompt.shape
    bs, s_len2, _ = x_sample.shape
    assert s_len == s_len2, "prompt and sample sequences are processed as one batch of sequences"
    nb = bp + bs
    t = nb * s_len
    dc = conv_a_w.shape[2]
    dr = conv_b_w.shape[2]
    heads, rb = lru_w_gate.shape[3], lru_w_gate.shape[4]
    f = ffn_w_down.shape[2]
    n_in = w_in.shape[2]
    assert n_in == 3 * dc + 2 * dr + 2 * d and dr % dc == 0 and d % dc == 0 and dc % rb == 0
    assert heads * rb == dr and s_len % (SUBLANES * SUBLANES) == 0

    fc = tiles.get("fc", 512)
    fp = _round_up(f, fc)
    tm_ffn = tiles.get("tm_ffn", _pick_tile(t, 1024))
    tm_in = tiles.get("tm_in", _pick_tile(t, 1024))
    tm_out = tiles.get("tm_out", _pick_tile(t, 256))
    seq_rows = tiles.get("seq_rows", _pick_tile(s_len, 256))
    wa = tiles.get("wa", min(dc, 256))

    gate_w, up_w = ffn_w_gate_up[..., :f], ffn_w_gate_up[..., f:]
    pad_cols = [(0, 0)] * 3 + [(0, fp - f)]
    w_gu = jnp.concatenate([jnp.pad(gate_w, pad_cols), jnp.pad(up_w, pad_cols)],
                           axis=-1).astype(BF16)
    w_d = jnp.pad(ffn_w_down, [(0, 0), (0, 0), (0, fp - f), (0, 0)]).astype(BF16)
    w_in_b = w_in.astype(BF16)
    half_w_gate = (0.5 * jnp.transpose(lru_w_gate, (0, 3, 4, 1, 2, 5))).reshape(
        depth, heads, rb, 4 * rb).astype(BF16)
    half_b_gate = 0.5 * lru_b_gate.reshape(depth, 4, dr)
    w_out_a_b = w_out_a.astype(BF16)
    w_out_b_b = w_out_b.astype(BF16)
    w_o_b = w_o.astype(BF16)
    ln_g = ln_gain[:, :, None, :]
    ln_b = ln_bias[:, :, None, :]
    b_in3 = b_in[:, None, :]
    conv_b_b3 = conv_b_b[:, None, :]
    b_o3 = b_o[:, None, :]

    x0 = _interleave_segments(jnp.concatenate([x_prompt, x_sample], axis=0)).reshape(t, d)

    def layer(l, x):
        def sel(k, ln):
            return jnp.stack([l, jnp.int32(k), jnp.int32(ln)]).astype(jnp.int32)

        x = _ffn_call(x, sel(0, 0), w_gu, w_d, ln_g, ln_b, alpha=alpha, tm=tm_ffn, fc=fc)
        proj = _inproj_call(x, sel(0, 0), w_in_b, b_in3, tm=tm_in, tn=dc)
        proj3 = proj.reshape(nb, s_len, n_in)
        ya_pre = _conv_a_call(proj3, sel(0, 0), conv_a_w, wa=wa, rows_per_block=seq_rows)
        yb_pre = _lru_call(proj3, sel(0, 0), conv_b_w, conv_b_b3, half_w_gate, half_b_gate,
                           lru_lambda, wcol=dc, rows_per_block=seq_rows)
        x = _mixer_out_call(x, sel(0, 1), ya_pre.reshape(t, dc), yb_pre.reshape(t, dr), proj,
                            w_out_a_b, w_out_b_b, w_o_b, b_o3, ln_g, ln_b, alpha=alpha, tm=tm_out)
        x = _ffn_call(x, sel(1, 2), w_gu, w_d, ln_g, ln_b, alpha=alpha, tm=tm_ffn, fc=fc)
        return x

    x = lax.fori_loop(0, depth, layer, x0)
    x = x.reshape(nb, s_len, d)
    return _deinterleave_segments(x[:bp]), _deinterleave_segments(x[bp:])


def kernel(x_prompt, x_sample, ln_gain, ln_bias, ffn_w_gate_up, ffn_w_down, w_in, b_in, conv_a_w, w_out_a, conv_b_w, conv_b_b, lru_w_gate, lru_b_gate, lru_lambda, w_out_b, w_o, b_o):
    return _trunk(x_prompt, x_sample, ln_gain, ln_bias, ffn_w_gate_up, ffn_w_down, w_in, b_in,
                  conv_a_w, w_out_a, conv_b_w, conv_b_b, lru_w_gate, lru_b_gate, lru_lambda,
                  w_out_b, w_o, b_o)
```

```python
import functools

import jax
import jax.numpy as jnp
from jax import lax
from jax.experimental import pallas as pl
from jax.experimental.pallas import tpu as pltpu

LN_EPS = 1e-5
LRU_C = 8.0
SUBLANES = 8
LOG2_E = 1.4426950408889634
GELU_C0 = 0.7978845608028654
GELU_C1 = GELU_C0 * 0.044715
V7X_VMEM_LIMIT_BYTES = 60000 * 1024

BF16 = jnp.bfloat16
F32 = jnp.float32


def _round_up(x, m):
    return (x + m - 1) // m * m


def _dot(a, b):
    return jnp.dot(a, b, preferred_element_type=F32)


def _layer_norm(y, g, b):
    mu = jnp.mean(y, axis=-1, keepdims=True)
    yc = y - mu
    var = jnp.mean(yc * yc, axis=-1, keepdims=True)
    return yc * lax.rsqrt(var + LN_EPS) * g + b


def _compiler_params(semantics):
    return pltpu.CompilerParams(dimension_semantics=semantics,
                                vmem_limit_bytes=V7X_VMEM_LIMIT_BYTES)


def _ffn_kernel(s_ref, x_ref, wg_ref, wu_ref, wd_ref, g_ref, b_ref, o_ref, xb_ref, *,
                alpha, row_block):
    del s_ref
    j = pl.program_id(1)
    last = pl.num_programs(1) - 1
    tm = x_ref.shape[0]

    def down(xb):
        gate = _dot(xb, wg_ref[...])
        up = _dot(xb, wu_ref[...])
        hidden = (gate * jax.nn.sigmoid(gate)) * up
        return _dot(hidden.astype(BF16), wd_ref[...])

    @pl.when(j == 0)
    def _():
        for r in range(0, tm, row_block):
            rows = pl.ds(r, row_block)
            xb = x_ref[rows, :].astype(BF16)
            xb_ref[rows, :] = xb
            o_ref[rows, :] = down(xb)

    @pl.when(jnp.logical_and(j > 0, j < last))
    def _():
        o_ref[...] += down(xb_ref[...])

    @pl.when(j == last)
    def _():
        for r in range(0, tm, row_block):
            rows = pl.ds(r, row_block)
            acc = o_ref[rows, :] + down(xb_ref[rows, :])
            y = alpha * x_ref[rows, :] + 0.5 * acc
            o_ref[rows, :] = _layer_norm(y, g_ref[...], b_ref[...])


def _ffn_call(x, sel, w_gu, w_d, ln_g, ln_b, *, alpha, tm, fc):
    t, d = x.shape
    fp = w_d.shape[2]
    nf = fp // fc
    assert nf >= 2
    grid_spec = pltpu.PrefetchScalarGridSpec(
        num_scalar_prefetch=1,
        grid=(t // tm, nf),
        in_specs=[
            pl.BlockSpec((tm, d), lambda i, j, s: (i, 0)),
            pl.BlockSpec((None, None, d, fc), lambda i, j, s: (s[0], s[1], 0, j)),
            pl.BlockSpec((None, None, d, fc), lambda i, j, s: (s[0], s[1], 0, nf + j)),
            pl.BlockSpec((None, None, fc, d), lambda i, j, s: (s[0], s[1], j, 0)),
            pl.BlockSpec((None, None, 1, d), lambda i, j, s: (s[0], s[2], 0, 0)),
            pl.BlockSpec((None, None, 1, d), lambda i, j, s: (s[0], s[2], 0, 0)),
        ],
        out_specs=pl.BlockSpec((tm, d), lambda i, j, s: (i, 0)),
        scratch_shapes=[pltpu.VMEM((tm, d), BF16)],
    )
    return pl.pallas_call(
        functools.partial(_ffn_kernel, alpha=alpha, row_block=min(tm, 256)),
        grid_spec=grid_spec,
        out_shape=jax.ShapeDtypeStruct((t, d), F32),
        compiler_params=_compiler_params(("parallel", "arbitrary")),
        name="ffn",
    )(sel, x, w_gu, w_gu, w_d, ln_g, ln_b)


def _inproj_kernel(s_ref, x_ref, w_ref, b_ref, o_ref, xb_ref):
    del s_ref

    @pl.when(pl.program_id(1) == 0)
    def _():
        xb_ref[...] = x_ref[...].astype(BF16)

    o_ref[...] = _dot(xb_ref[...], w_ref[...]) + b_ref[...]


def _inproj_call(x, sel, w_in, b_in, *, tm, tn):
    t, d = x.shape
    n = w_in.shape[2]
    grid_spec = pltpu.PrefetchScalarGridSpec(
        num_scalar_prefetch=1,
        grid=(t // tm, n // tn),
        in_specs=[
            pl.BlockSpec((tm, d), lambda i, j, s: (i, 0)),
            pl.BlockSpec((None, d, tn), lambda i, j, s: (s[0], 0, j)),
            pl.BlockSpec((None, 1, tn), lambda i, j, s: (s[0], 0, j)),
        ],
        out_specs=pl.BlockSpec((tm, tn), lambda i, j, s: (i, j)),
        scratch_shapes=[pltpu.VMEM((tm, d), BF16)],
    )
    return pl.pallas_call(
        _inproj_kernel,
        grid_spec=grid_spec,
        out_shape=jax.ShapeDtypeStruct((t, n), F32),
        compiler_params=_compiler_params(("parallel", "arbitrary")),
        name="inproj",
    )(sel, x, w_in, b_in)


def _seq_start_tile(tile_of_last_steps):
    sub = lax.broadcasted_iota(jnp.int32, tile_of_last_steps.shape, 0)
    return jnp.where(sub == 0, 0.0, pltpu.roll(tile_of_last_steps, 1, 0))


def _seq_end_tile(tile_of_first_steps):
    sub = lax.broadcasted_iota(jnp.int32, tile_of_first_steps.shape, 0)
    return jnp.where(sub == SUBLANES - 1, 0.0, pltpu.roll(tile_of_first_steps, SUBLANES - 1, 0))


def _tile_at(load_tile, r, is_edge, edge_tile):
    return jnp.where(is_edge, edge_tile, load_tile(r))


def _segment_carries(p_end, h_end, reverse):
    sub = lax.broadcasted_iota(jnp.int32, p_end.shape, 0)
    a, u = p_end, h_end
    for sh in (1, 2, 4):
        if reverse:
            a_o = pltpu.roll(a, SUBLANES - sh, 0)
            u_o = pltpu.roll(u, SUBLANES - sh, 0)
            m = sub < SUBLANES - sh
        else:
            a_o = pltpu.roll(a, sh, 0)
            u_o = pltpu.roll(u, sh, 0)
            m = sub >= sh
        u = jnp.where(m, a * u_o + u, u)
        a = jnp.where(m, a * a_o, a)
    return _seq_end_tile(u) if reverse else _seq_start_tile(u)


def _lru_kernel(s_ref, x_ref, y_ref, cw_ref, cb_ref, wg_ref, bg_ref, lam_ref, o_ref,
                af_s, uf_s, ab_s, ub_s, hf_s, *, rows_per_block, nsplit):
    del s_ref
    s_len, w = x_ref.shape
    nr = rows_per_block
    nblk = s_len // nr
    tiles_per_split = s_len // SUBLANES // nsplit
    pairs_per_split = tiles_per_split // 2
    split_rows = tiles_per_split * SUBLANES

    def range_tile(q, t, ntiles=1):
        base = pl.multiple_of(t * SUBLANES, ntiles * SUBLANES)
        return pl.ds(base + q * split_rows, ntiles * SUBLANES)

    def x_tile(r):
        return x_ref[pl.ds(pl.multiple_of(r, SUBLANES), SUBLANES), :]

    before_first_1 = _seq_start_tile(x_tile(s_len - SUBLANES))
    before_first_2 = _seq_start_tile(x_tile(s_len - 2 * SUBLANES))
    after_last = _seq_end_tile(x_tile(0))
    cw = cw_ref[...]
    cw0, cw1, cw2, cw3 = cw[0:1, :], cw[1:2, :], cw[2:3, :], cw[3:4, :]
    cb = cb_ref[...]
    bg = bg_ref[...]
    neg_lam = -lam_ref[...]
    log_a_scale = (-LRU_C) * (jnp.maximum(neg_lam, 0.0) + jnp.log1p(jnp.exp(-jnp.abs(neg_lam))))
    half_neg_scale = -0.5 * log_a_scale
    half_scale_log2 = (0.5 * LOG2_E) * log_a_scale

    def gates_block(k, carry):
        r0 = pl.multiple_of(k * nr, SUBLANES)
        rows = pl.ds(r0, nr)
        x = x_ref[rows, :]
        m1 = _tile_at(x_tile, jnp.maximum(r0 - SUBLANES, 0), k == 0, before_first_1)
        m2 = _tile_at(x_tile, jnp.maximum(r0 - 2 * SUBLANES, 0), k == 0, before_first_2)
        p1 = _tile_at(x_tile, jnp.minimum(r0 + nr, s_len - SUBLANES), k == nblk - 1, after_last)
        x_m1 = jnp.concatenate([m1, x[:nr - SUBLANES]], axis=0)
        x_m2 = jnp.concatenate([m2, m1, x[:nr - 2 * SUBLANES]], axis=0)
        x_p1 = jnp.concatenate([x[SUBLANES:], p1], axis=0)
        xc = cw0 * x_m2 + cw1 * x_m1 + cw2 * x + cw3 * x_p1 + cb
        half_xc = 0.5 * xc
        g = _dot(xc.astype(BF16), wg_ref[...])
        for d, (a_s, u_s) in enumerate(((af_s, uf_s), (ab_s, ub_s))):
            t_r = jnp.tanh(g[:, (2 * d) * w:(2 * d + 1) * w] + bg[2 * d:2 * d + 1, :])
            t_i = jnp.tanh(g[:, (2 * d + 1) * w:(2 * d + 2) * w] + bg[2 * d + 1:2 * d + 2, :])
            c_neg = half_neg_scale[d:d + 1, :]
            c_log2 = half_scale_log2[d:d + 1, :]
            t_r1 = t_r + 1.0
            a = jnp.exp2(t_r1 * c_log2)
            one_minus_a2 = jnp.tanh(t_r1 * c_neg) * (a * a + 1.0)
            root = jnp.where(one_minus_a2 > 0.0, one_minus_a2 * lax.rsqrt(one_minus_a2), 0.0)
            a_s[rows, :] = a
            u_s[rows, :] = root * (t_i * half_xc + half_xc)
        return carry

    lax.fori_loop(0, nblk, gates_block, 0, unroll=2)

    lo_half, hi_half = slice(0, SUBLANES), slice(SUBLANES, 2 * SUBLANES)

    def range_summary(j, carry):
        out = []
        for q in range(nsplit):
            hf, pf, hb, pb = carry[q]
            rf = range_tile(q, 2 * j, ntiles=2)
            rb = range_tile(q, tiles_per_split - 2 - 2 * j, ntiles=2)
            a_f, u_f = af_s[rf, :], uf_s[rf, :]
            a_b, u_b = ab_s[rb, :], ub_s[rb, :]
            hf = a_f[hi_half] * (a_f[lo_half] * hf + u_f[lo_half]) + u_f[hi_half]
            pf = a_f[hi_half] * (a_f[lo_half] * pf)
            hb = a_b[lo_half] * (a_b[hi_half] * hb + u_b[hi_half]) + u_b[lo_half]
            pb = a_b[lo_half] * (a_b[hi_half] * pb)
            out.append((hf, pf, hb, pb))
        return tuple(out)

    zeros = jnp.zeros((SUBLANES, w), F32)
    ones = jnp.ones((SUBLANES, w), F32)
    ends = lax.fori_loop(0, pairs_per_split, range_summary,
                         ((zeros, ones, zeros, ones),) * nsplit, unroll=2)
    h_f = [e[0] for e in ends]
    p_f = [e[1] for e in ends]
    h_b = [e[2] for e in ends]
    p_b = [e[3] for e in ends]

    p_seg, h_seg = p_f[0], h_f[0]
    for q in range(1, nsplit):
        h_seg = h_f[q] + p_f[q] * h_seg
        p_seg = p_f[q] * p_seg
    carry_f = [_segment_carries(p_seg, h_seg, reverse=False)]
    for q in range(nsplit - 1):
        carry_f.append(h_f[q] + p_f[q] * carry_f[q])
    p_seg, h_seg = p_b[nsplit - 1], h_b[nsplit - 1]
    for q in range(nsplit - 2, -1, -1):
        h_seg = h_b[q] + p_b[q] * h_seg
        p_seg = p_b[q] * p_seg
    carry_b = [None] * nsplit
    carry_b[nsplit - 1] = _segment_carries(p_seg, h_seg, reverse=True)
    for q in range(nsplit - 1, 0, -1):
        carry_b[q - 1] = h_b[q] + p_b[q] * carry_b[q]

    def forward_states(j, carry):
        out = []
        for q in range(nsplit):
            rf = range_tile(q, 2 * j, ntiles=2)
            a_f, u_f = af_s[rf, :], uf_s[rf, :]
            hf_lo = a_f[lo_half] * carry[q] + u_f[lo_half]
            hf_hi = a_f[hi_half] * hf_lo + u_f[hi_half]
            hf_s[rf, :] = jnp.concatenate([hf_lo, hf_hi], axis=0)
            out.append(hf_hi)
        return tuple(out)

    lax.fori_loop(0, pairs_per_split, forward_states, tuple(carry_f), unroll=2)

    def backward_states_and_output(j, carry):
        out = []
        for q in range(nsplit):
            rows = range_tile(q, tiles_per_split - 2 - 2 * j, ntiles=2)
            a_b, u_b = ab_s[rows, :], ub_s[rows, :]
            hb_hi = a_b[hi_half] * carry[q] + u_b[hi_half]
            hb_lo = a_b[lo_half] * hb_hi + u_b[lo_half]
            h = hf_s[rows, :] + jnp.concatenate([hb_lo, hb_hi], axis=0)
            y = y_ref[rows, :]
            half_y = 0.5 * y
            t = jnp.tanh(y * (GELU_C0 + GELU_C1 * (y * y)))
            o_ref[rows, :] = (h * (half_y * t + half_y)).astype(o_ref.dtype)
            out.append(hb_lo)
        return tuple(out)

    lax.fori_loop(0, pairs_per_split, backward_states_and_output, tuple(carry_b), unroll=2)


def _lru_call(proj, sel, conv_b_w, conv_b_b, half_w_gate, half_b_gate, lam, *, wcol,
              rows_per_block):
    nb, s_len, _ = proj.shape
    dr = conv_b_w.shape[2]
    heads, rb = half_w_gate.shape[1], half_w_gate.shape[2]
    assert (3 * wcol) % rb == 0 and 3 * SUBLANES <= rows_per_block <= s_len
    ntiles = s_len // SUBLANES
    nsplit = max(q for q in (1, 2, 4) if ntiles % (2 * q) == 0)
    x_col0 = 3 * wcol // rb
    y_col0 = x_col0 + heads
    grid_spec = pltpu.PrefetchScalarGridSpec(
        num_scalar_prefetch=1,
        grid=(nb, heads),
        in_specs=[
            pl.BlockSpec((None, s_len, rb), lambda n, h, s: (n, 0, x_col0 + h)),
            pl.BlockSpec((None, s_len, rb), lambda n, h, s: (n, 0, y_col0 + h)),
            pl.BlockSpec((None, 4, rb), lambda n, h, s: (s[0], 0, h)),
            pl.BlockSpec((None, 1, rb), lambda n, h, s: (s[0], 0, h)),
            pl.BlockSpec((None, None, rb, 4 * rb), lambda n, h, s: (s[0], h, 0, 0)),
            pl.BlockSpec((None, 4, rb), lambda n, h, s: (s[0], 0, h)),
            pl.BlockSpec((None, 2, rb), lambda n, h, s: (s[0], 0, h)),
        ],
        out_specs=pl.BlockSpec((None, s_len, rb), lambda n, h, s: (n, 0, h)),
        scratch_shapes=[pltpu.VMEM((s_len, rb), F32)] * 5,
    )
    return pl.pallas_call(
        functools.partial(_lru_kernel, rows_per_block=rows_per_block, nsplit=nsplit),
        grid_spec=grid_spec,
        out_shape=jax.ShapeDtypeStruct((nb, s_len, dr), BF16),
        compiler_params=_compiler_params(("parallel", "parallel")),
        name="lru",
    )(sel, proj, proj, conv_b_w, conv_b_b, half_w_gate, half_b_gate, lam)


def _mixer_out_kernel(s_ref, x_ref, c_ref, b_ref, v_ref, cp_ref, vp_ref, cn_ref, vn_ref, yb_ref,
                      *rest, alpha, ncol, blocks_per_seq):
    del s_ref
    ga_refs = rest[:ncol]
    gb_refs = rest[ncol:2 * ncol]
    cw_ref, woa_ref, wob_ref, wo_ref, bo_ref, g_ref, bb_ref, o_ref = rest[2 * ncol:]
    tm = x_ref.shape[0]
    block_in_seq = pl.program_id(0) % blocks_per_seq

    cv = c_ref[...] * v_ref[...]
    prev_tile = cp_ref[...] * vp_ref[...]
    next_tile = cn_ref[...] * vn_ref[...]
    prev_tile = jnp.where(block_in_seq == 0, _seq_start_tile(prev_tile), prev_tile)
    next_tile = jnp.where(block_in_seq == blocks_per_seq - 1, _seq_end_tile(next_tile), next_tile)
    cv_m1 = jnp.concatenate([prev_tile, cv[:tm - SUBLANES]], axis=0)
    cv_p1 = jnp.concatenate([cv[SUBLANES:], next_tile], axis=0)
    cw = cw_ref[...]
    ya_pre = b_ref[...] * (cw[0:1, :] * cv_m1 + cw[1:2, :] * cv + cw[2:3, :] * cv_p1)

    y_a = _dot(ya_pre.astype(BF16), woa_ref[...])
    y_b = _dot(yb_ref[...], wob_ref[...])
    w = ga_refs[0].shape[1]
    parts = []
    for c in range(ncol):
        sl = slice(c * w, (c + 1) * w)
        parts.append(jax.nn.sigmoid(ga_refs[c][...]) * y_a[:, sl]
                     + jax.nn.sigmoid(gb_refs[c][...]) * y_b[:, sl])
    merged = jnp.concatenate(parts, axis=1).astype(BF16)
    out = _dot(merged, wo_ref[...]) + bo_ref[...]
    o_ref[...] = _layer_norm(alpha * x_ref[...] + out, g_ref[...], bb_ref[...])


def _mixer_out_call(x, sel, yb_pre, proj, conv_a_w, w_out_a, w_out_b, w_o, b_o, ln_g, ln_b, *,
                    alpha, tm, s_len):
    t, d = x.shape
    dc = conv_a_w.shape[2]
    dr = yb_pre.shape[1]
    w = dc
    ncol = d // w
    ga_col0 = 3 + 2 * (dr // w)
    gb_col0 = ga_col0 + ncol
    assert s_len % tm == 0
    blocks_per_seq = s_len // tm
    tiles_per_block = tm // SUBLANES
    tiles_per_seq = s_len // SUBLANES
    resident = pl.Buffered(1)

    def col_spec(col):
        return pl.BlockSpec((tm, w), lambda i, s: (i, col))

    def neighbour_tile_spec(col, offset):
        def index(i, s):
            seq = i // blocks_per_seq
            tile = ((i % blocks_per_seq) * tiles_per_block + offset + tiles_per_seq) % tiles_per_seq
            return (seq * tiles_per_seq + tile, col)
        return pl.BlockSpec((SUBLANES, w), index)

    in_specs = (
        [pl.BlockSpec((tm, d), lambda i, s: (i, 0)),
         col_spec(0), col_spec(1), col_spec(2),
         neighbour_tile_spec(0, -1), neighbour_tile_spec(2, -1),
         neighbour_tile_spec(0, tiles_per_block), neighbour_tile_spec(2, tiles_per_block),
         pl.BlockSpec((tm, dr), lambda i, s: (i, 0))]
        + [col_spec(ga_col0 + c) for c in range(ncol)]
        + [col_spec(gb_col0 + c) for c in range(ncol)]
        + [pl.BlockSpec((None, 3, dc), lambda i, s: (s[0], 0, 0)),
           pl.BlockSpec((None, dc, d), lambda i, s: (s[0], 0, 0), pipeline_mode=resident),
           pl.BlockSpec((None, dr, d), lambda i, s: (s[0], 0, 0), pipeline_mode=resident),
           pl.BlockSpec((None, d, d), lambda i, s: (s[0], 0, 0), pipeline_mode=resident),
           pl.BlockSpec((None, 1, d), lambda i, s: (s[0], 0, 0)),
           pl.BlockSpec((None, None, 1, d), lambda i, s: (s[0], s[2], 0, 0)),
           pl.BlockSpec((None, None, 1, d), lambda i, s: (s[0], s[2], 0, 0))])
    grid_spec = pltpu.PrefetchScalarGridSpec(
        num_scalar_prefetch=1,
        grid=(t // tm,),
        in_specs=in_specs,
        out_specs=pl.BlockSpec((tm, d), lambda i, s: (i, 0)),
    )
    return pl.pallas_call(
        functools.partial(_mixer_out_kernel, alpha=alpha, ncol=ncol, blocks_per_seq=blocks_per_seq),
        grid_spec=grid_spec,
        out_shape=jax.ShapeDtypeStruct((t, d), F32),
        compiler_params=_compiler_params(("parallel",)),
        name="mixer_out",
    )(sel, x, *([proj] * 7), yb_pre, *([proj] * (2 * ncol)), conv_a_w, w_out_a, w_out_b, w_o, b_o,
      ln_g, ln_b)


def _pick_tile(n, target):
    if n <= target:
        return n
    for cand in range(target, 7, -1):
        if n % cand == 0 and cand % SUBLANES == 0:
            return cand
    return n


def _interleave_segments(x):
    nb, s_len, d = x.shape
    return x.reshape(nb, SUBLANES, s_len // SUBLANES, d).transpose(0, 2, 1, 3).reshape(nb, s_len, d)


def _deinterleave_segments(x):
    nb, s_len, d = x.shape
    return x.reshape(nb, s_len // SUBLANES, SUBLANES, d).transpose(0, 2, 1, 3).reshape(nb, s_len, d)


def _trunk(x_prompt, x_sample, ln_gain, ln_bias, ffn_w_gate_up, ffn_w_down, w_in, b_in,
           conv_a_w, w_out_a, conv_b_w, conv_b_b, lru_w_gate, lru_b_gate, lru_lambda,
           w_out_b, w_o, b_o, *, tiles=None):
    tiles = dict(tiles or {})
    depth = ln_gain.shape[0]
    alpha = (2.0 * depth) ** 0.25
    bp, s_len, d = x_prompt.shape
    bs, s_len2, _ = x_sample.shape
    assert s_len == s_len2, "prompt and sample sequences are processed as one batch of sequences"
    nb = bp + bs
    t = nb * s_len
    dc = conv_a_w.shape[2]
    dr = conv_b_w.shape[2]
    heads, rb = lru_w_gate.shape[3], lru_w_gate.shape[4]
    f = ffn_w_down.shape[2]
    n_in = w_in.shape[2]
    assert n_in == 3 * dc + 2 * dr + 2 * d and dr % dc == 0 and d % dc == 0 and dc % rb == 0
    assert heads * rb == dr and s_len % (SUBLANES * SUBLANES) == 0

    fc = tiles.get("fc", 512)
    fp = _round_up(f, fc)
    tm_ffn = tiles.get("tm_ffn", _pick_tile(t, 1024))
    tm_in = tiles.get("tm_in", _pick_tile(t, 1024))
    tm_out = tiles.get("tm_out", _pick_tile(s_len, 256))
    seq_rows = tiles.get("seq_rows", _pick_tile(s_len, 256))

    gate_w, up_w = ffn_w_gate_up[..., :f], ffn_w_gate_up[..., f:]
    pad_cols = [(0, 0)] * 3 + [(0, fp - f)]
    w_gu = jnp.concatenate([jnp.pad(gate_w, pad_cols), jnp.pad(up_w, pad_cols)],
                           axis=-1).astype(BF16)
    w_d = jnp.pad(ffn_w_down, [(0, 0), (0, 0), (0, fp - f), (0, 0)]).astype(BF16)
    w_in_b = w_in.astype(BF16)
    half_w_gate = (0.5 * jnp.transpose(lru_w_gate, (0, 3, 4, 1, 2, 5))).reshape(
        depth, heads, rb, 4 * rb).astype(BF16)
    half_b_gate = 0.5 * lru_b_gate.reshape(depth, 4, dr)
    w_out_a_b = w_out_a.astype(BF16)
    w_out_b_b = w_out_b.astype(BF16)
    w_o_b = w_o.astype(BF16)
    ln_g = ln_gain[:, :, None, :]
    ln_b = ln_bias[:, :, None, :]
    b_in3 = b_in[:, None, :]
    conv_b_b3 = conv_b_b[:, None, :]
    b_o3 = b_o[:, None, :]

    x0 = _interleave_segments(jnp.concatenate([x_prompt, x_sample], axis=0)).reshape(t, d)

    def layer(l, x):
        def sel(k, ln):
            return jnp.stack([l, jnp.int32(k), jnp.int32(ln)]).astype(jnp.int32)

        x = _ffn_call(x, sel(0, 0), w_gu, w_d, ln_g, ln_b, alpha=alpha, tm=tm_ffn, fc=fc)
        proj = _inproj_call(x, sel(0, 0), w_in_b, b_in3, tm=tm_in, tn=dc)
        yb_pre = _lru_call(proj.reshape(nb, s_len, n_in), sel(0, 0), conv_b_w, conv_b_b3,
                           half_w_gate, half_b_gate, lru_lambda, wcol=dc, rows_per_block=seq_rows)
        x = _mixer_out_call(x, sel(0, 1), yb_pre.reshape(t, dr), proj, conv_a_w,
                            w_out_a_b, w_out_b_b, w_o_b, b_o3, ln_g, ln_b, alpha=alpha, tm=tm_out,
                            s_len=s_len)
        x = _ffn_call(x, sel(1, 2), w_gu, w_d, ln_g, ln_b, alpha=alpha, tm=tm_ffn, fc=fc)
        return x

    x = lax.fori_loop(0, depth, layer, x0)
    x = x.reshape(nb, s_len, d)
    return _deinterleave_segments(x[:bp]), _deinterleave_segments(x[bp:])


def kernel(x_prompt, x_sample, ln_gain, ln_bias, ffn_w_gate_up, ffn_w_down, w_in, b_in, conv_a_w, w_out_a, conv_b_w, conv_b_b, lru_w_gate, lru_b_gate, lru_lambda, w_out_b, w_o, b_o):
    return _trunk(x_prompt, x_sample, ln_gain, ln_bias, ffn_w_gate_up, ffn_w_down, w_in, b_in,
                  conv_a_w, w_out_a, conv_b_w, conv_b_b, lru_w_gate, lru_b_gate, lru_lambda,
                  w_out_b, w_o, b_o)
```

```python
import functools

import jax
import jax.numpy as jnp
from jax import lax
from jax.experimental import pallas as pl
from jax.experimental.pallas import tpu as pltpu

LN_EPS = 1e-5
LRU_C = 8.0
SUBLANES = 8
LOG2_E = 1.4426950408889634
GELU_C0 = 0.7978845608028654
GELU_C1 = GELU_C0 * 0.044715
V7X_VMEM_LIMIT_BYTES = 60000 * 1024

BF16 = jnp.bfloat16
F32 = jnp.float32


def _round_up(x, m):
    return (x + m - 1) // m * m


def _dot(a, b):
    return jnp.dot(a, b, preferred_element_type=F32)


def _layer_norm(y, g, b):
    mu = jnp.mean(y, axis=-1, keepdims=True)
    yc = y - mu
    var = jnp.mean(yc * yc, axis=-1, keepdims=True)
    return yc * lax.rsqrt(var + LN_EPS) * g + b


def _compiler_params(semantics):
    return pltpu.CompilerParams(dimension_semantics=semantics,
                                vmem_limit_bytes=V7X_VMEM_LIMIT_BYTES)


def _ffn_kernel(s_ref, x_ref, wg_ref, wu_ref, wd_ref, g_ref, b_ref, o_ref, xb_ref, *,
                alpha, row_block):
    del s_ref
    j = pl.program_id(1)
    last = pl.num_programs(1) - 1
    tm = x_ref.shape[0]

    def down(xb):
        gate = _dot(xb, wg_ref[...])
        up = _dot(xb, wu_ref[...])
        hidden = (gate * jax.nn.sigmoid(gate)) * up
        return _dot(hidden.astype(BF16), wd_ref[...])

    @pl.when(j == 0)
    def _():
        for r in range(0, tm, row_block):
            rows = pl.ds(r, row_block)
            xb = x_ref[rows, :].astype(BF16)
            xb_ref[rows, :] = xb
            o_ref[rows, :] = down(xb)

    @pl.when(jnp.logical_and(j > 0, j < last))
    def _():
        o_ref[...] += down(xb_ref[...])

    @pl.when(j == last)
    def _():
        for r in range(0, tm, row_block):
            rows = pl.ds(r, row_block)
            acc = o_ref[rows, :] + down(xb_ref[rows, :])
            y = alpha * x_ref[rows, :] + 0.5 * acc
            o_ref[rows, :] = _layer_norm(y, g_ref[...], b_ref[...])


def _ffn_call(x, sel, w_g, w_u, w_d, ln_g, ln_b, *, alpha, tm, fc):
    t, d = x.shape
    fp = w_d.shape[2]
    nf = fp // fc
    assert nf >= 2
    grid_spec = pltpu.PrefetchScalarGridSpec(
        num_scalar_prefetch=1,
        grid=(t // tm, nf),
        in_specs=[
            pl.BlockSpec((tm, d), lambda i, j, s: (i, 0)),
            pl.BlockSpec((None, None, d, fc), lambda i, j, s: (s[0], s[1], 0, j)),
            pl.BlockSpec((None, None, d, fc), lambda i, j, s: (s[0], s[1], 0, j)),
            pl.BlockSpec((None, None, fc, d), lambda i, j, s: (s[0], s[1], j, 0)),
            pl.BlockSpec((None, None, 1, d), lambda i, j, s: (s[0], s[2], 0, 0)),
            pl.BlockSpec((None, None, 1, d), lambda i, j, s: (s[0], s[2], 0, 0)),
        ],
        out_specs=pl.BlockSpec((tm, d), lambda i, j, s: (i, 0)),
        scratch_shapes=[pltpu.VMEM((tm, d), BF16)],
    )
    return pl.pallas_call(
        functools.partial(_ffn_kernel, alpha=alpha, row_block=min(tm, 256)),
        grid_spec=grid_spec,
        out_shape=jax.ShapeDtypeStruct((t, d), F32),
        compiler_params=_compiler_params(("parallel", "arbitrary")),
        name="ffn",
    )(sel, x, w_g, w_u, w_d, ln_g, ln_b)


def _inproj_kernel(s_ref, x_ref, w_ref, b_ref, o_ref, xb_ref):
    del s_ref

    @pl.when(pl.program_id(1) == 0)
    def _():
        xb_ref[...] = x_ref[...].astype(BF16)

    o_ref[...] = _dot(xb_ref[...], w_ref[...]) + b_ref[...]


def _inproj_call(x, sel, w_in, b_in, *, tm, tn, first_chunk, nchunks):
    t, d = x.shape
    grid_spec = pltpu.PrefetchScalarGridSpec(
        num_scalar_prefetch=1,
        grid=(t // tm, nchunks),
        in_specs=[
            pl.BlockSpec((tm, d), lambda i, j, s: (i, 0)),
            pl.BlockSpec((None, d, tn), lambda i, j, s: (s[0], 0, first_chunk + j)),
            pl.BlockSpec((None, 1, tn), lambda i, j, s: (s[0], 0, first_chunk + j)),
        ],
        out_specs=pl.BlockSpec((tm, tn), lambda i, j, s: (i, j)),
        scratch_shapes=[pltpu.VMEM((tm, d), BF16)],
    )
    return pl.pallas_call(
        _inproj_kernel,
        grid_spec=grid_spec,
        out_shape=jax.ShapeDtypeStruct((t, nchunks * tn), F32),
        compiler_params=_compiler_params(("parallel", "arbitrary")),
        name="inproj",
    )(sel, x, w_in, b_in)


def _seq_start_tile(tile_of_last_steps):
    sub = lax.broadcasted_iota(jnp.int32, tile_of_last_steps.shape, 0)
    return jnp.where(sub == 0, 0.0, pltpu.roll(tile_of_last_steps, 1, 0))


def _seq_end_tile(tile_of_first_steps):
    sub = lax.broadcasted_iota(jnp.int32, tile_of_first_steps.shape, 0)
    return jnp.where(sub == SUBLANES - 1, 0.0, pltpu.roll(tile_of_first_steps, SUBLANES - 1, 0))


def _tile_at(load_tile, r, is_edge, edge_tile):
    return jnp.where(is_edge, edge_tile, load_tile(r))


def _segment_carries(p_end, h_end, reverse):
    sub = lax.broadcasted_iota(jnp.int32, p_end.shape, 0)
    a, u = p_end, h_end
    for sh in (1, 2, 4):
        if reverse:
            a_o = pltpu.roll(a, SUBLANES - sh, 0)
            u_o = pltpu.roll(u, SUBLANES - sh, 0)
            m = sub < SUBLANES - sh
        else:
            a_o = pltpu.roll(a, sh, 0)
            u_o = pltpu.roll(u, sh, 0)
            m = sub >= sh
        u = jnp.where(m, a * u_o + u, u)
        a = jnp.where(m, a * a_o, a)
    return _seq_end_tile(u) if reverse else _seq_start_tile(u)


def _lru_kernel(s_ref, x_ref, y_ref, cw_ref, cb_ref, wg_ref, bg_ref, lam_ref, o_ref,
                af_s, uf_s, ab_s, ub_s, hf_s, *, rows_per_block, nsplit):
    del s_ref
    s_len, w = x_ref.shape
    nr = rows_per_block
    nblk = s_len // nr
    tiles_per_split = s_len // SUBLANES // nsplit
    pairs_per_split = tiles_per_split // 2
    split_rows = tiles_per_split * SUBLANES

    def range_tile(q, t, ntiles=1):
        base = pl.multiple_of(t * SUBLANES, ntiles * SUBLANES)
        return pl.ds(base + q * split_rows, ntiles * SUBLANES)

    def x_tile(r):
        return x_ref[pl.ds(pl.multiple_of(r, SUBLANES), SUBLANES), :]

    before_first_1 = _seq_start_tile(x_tile(s_len - SUBLANES))
    before_first_2 = _seq_start_tile(x_tile(s_len - 2 * SUBLANES))
    after_last = _seq_end_tile(x_tile(0))
    cw = cw_ref[...]
    cw0, cw1, cw2, cw3 = cw[0:1, :], cw[1:2, :], cw[2:3, :], cw[3:4, :]
    cb = cb_ref[...]
    bg = bg_ref[...]
    neg_lam = -lam_ref[...]
    log_a_scale = (-LRU_C) * (jnp.maximum(neg_lam, 0.0) + jnp.log1p(jnp.exp(-jnp.abs(neg_lam))))
    half_neg_scale = -0.5 * log_a_scale
    half_scale_log2 = (0.5 * LOG2_E) * log_a_scale

    def gates_block(k, carry):
        r0 = pl.multiple_of(k * nr, SUBLANES)
        rows = pl.ds(r0, nr)
        x = x_ref[rows, :]
        m1 = _tile_at(x_tile, jnp.maximum(r0 - SUBLANES, 0), k == 0, before_first_1)
        m2 = _tile_at(x_tile, jnp.maximum(r0 - 2 * SUBLANES, 0), k == 0, before_first_2)
        p1 = _tile_at(x_tile, jnp.minimum(r0 + nr, s_len - SUBLANES), k == nblk - 1, after_last)
        x_m1 = jnp.concatenate([m1, x[:nr - SUBLANES]], axis=0)
        x_m2 = jnp.concatenate([m2, m1, x[:nr - 2 * SUBLANES]], axis=0)
        x_p1 = jnp.concatenate([x[SUBLANES:], p1], axis=0)
        xc = cw0 * x_m2 + cw1 * x_m1 + cw2 * x + cw3 * x_p1 + cb
        half_xc = 0.5 * xc
        g = _dot(xc.astype(BF16), wg_ref[...])
        for d, (a_s, u_s) in enumerate(((af_s, uf_s), (ab_s, ub_s))):
            t_r = jnp.tanh(g[:, (2 * d) * w:(2 * d + 1) * w] + bg[2 * d:2 * d + 1, :])
            t_i = jnp.tanh(g[:, (2 * d + 1) * w:(2 * d + 2) * w] + bg[2 * d + 1:2 * d + 2, :])
            c_neg = half_neg_scale[d:d + 1, :]
            c_log2 = half_scale_log2[d:d + 1, :]
            t_r1 = t_r + 1.0
            a = jnp.exp2(t_r1 * c_log2)
            one_minus_a2 = jnp.tanh(t_r1 * c_neg) * (a * a + 1.0)
            root = jnp.where(one_minus_a2 > 0.0, one_minus_a2 * lax.rsqrt(one_minus_a2), 0.0)
            a_s[rows, :] = a
            u_s[rows, :] = root * (t_i * half_xc + half_xc)
        return carry

    lax.fori_loop(0, nblk, gates_block, 0, unroll=2)

    lo_half, hi_half = slice(0, SUBLANES), slice(SUBLANES, 2 * SUBLANES)

    def range_summary(j, carry):
        out = []
        for q in range(nsplit):
            hf, pf, hb, pb = carry[q]
            rf = range_tile(q, 2 * j, ntiles=2)
            rb = range_tile(q, tiles_per_split - 2 - 2 * j, ntiles=2)
            a_f, u_f = af_s[rf, :], uf_s[rf, :]
            a_b, u_b = ab_s[rb, :], ub_s[rb, :]
            hf = a_f[hi_half] * (a_f[lo_half] * hf + u_f[lo_half]) + u_f[hi_half]
            pf = a_f[hi_half] * (a_f[lo_half] * pf)
            hb = a_b[lo_half] * (a_b[hi_half] * hb + u_b[hi_half]) + u_b[lo_half]
            pb = a_b[lo_half] * (a_b[hi_half] * pb)
            out.append((hf, pf, hb, pb))
        return tuple(out)

    zeros = jnp.zeros((SUBLANES, w), F32)
    ones = jnp.ones((SUBLANES, w), F32)
    ends = lax.fori_loop(0, pairs_per_split, range_summary,
                         ((zeros, ones, zeros, ones),) * nsplit, unroll=2)
    h_f = [e[0] for e in ends]
    p_f = [e[1] for e in ends]
    h_b = [e[2] for e in ends]
    p_b = [e[3] for e in ends]

    p_seg, h_seg = p_f[0], h_f[0]
    for q in range(1, nsplit):
        h_seg = h_f[q] + p_f[q] * h_seg
        p_seg = p_f[q] * p_seg
    carry_f = [_segment_carries(p_seg, h_seg, reverse=False)]
    for q in range(nsplit - 1):
        carry_f.append(h_f[q] + p_f[q] * carry_f[q])
    p_seg, h_seg = p_b[nsplit - 1], h_b[nsplit - 1]
    for q in range(nsplit - 2, -1, -1):
        h_seg = h_b[q] + p_b[q] * h_seg
        p_seg = p_b[q] * p_seg
    carry_b = [None] * nsplit
    carry_b[nsplit - 1] = _segment_carries(p_seg, h_seg, reverse=True)
    for q in range(nsplit - 1, 0, -1):
        carry_b[q - 1] = h_b[q] + p_b[q] * carry_b[q]

    def forward_states(j, carry):
        out = []
        for q in range(nsplit):
            rf = range_tile(q, 2 * j, ntiles=2)
            a_f, u_f = af_s[rf, :], uf_s[rf, :]
            hf_lo = a_f[lo_half] * carry[q] + u_f[lo_half]
            hf_hi = a_f[hi_half] * hf_lo + u_f[hi_half]
            hf_s[rf, :] = jnp.concatenate([hf_lo, hf_hi], axis=0)
            out.append(hf_hi)
        return tuple(out)

    lax.fori_loop(0, pairs_per_split, forward_states, tuple(carry_f), unroll=2)

    def backward_states_and_output(j, carry):
        out = []
        for q in range(nsplit):
            rows = range_tile(q, tiles_per_split - 2 - 2 * j, ntiles=2)
            a_b, u_b = ab_s[rows, :], ub_s[rows, :]
            hb_hi = a_b[hi_half] * carry[q] + u_b[hi_half]
            hb_lo = a_b[lo_half] * hb_hi + u_b[lo_half]
            h = hf_s[rows, :] + jnp.concatenate([hb_lo, hb_hi], axis=0)
            y = y_ref[rows, :]
            half_y = 0.5 * y
            t = jnp.tanh(y * (GELU_C0 + GELU_C1 * (y * y)))
            o_ref[rows, :] = (h * (half_y * t + half_y)).astype(o_ref.dtype)
            out.append(hb_lo)
        return tuple(out)

    lax.fori_loop(0, pairs_per_split, backward_states_and_output, tuple(carry_b), unroll=2)


def _lru_call(proj, sel, conv_b_w, conv_b_b, half_w_gate, half_b_gate, lam, *, wcol,
              rows_per_block):
    nb, s_len, _ = proj.shape
    dr = conv_b_w.shape[2]
    heads, rb = half_w_gate.shape[1], half_w_gate.shape[2]
    assert (3 * wcol) % rb == 0 and 3 * SUBLANES <= rows_per_block <= s_len
    ntiles = s_len // SUBLANES
    nsplit = max(q for q in (1, 2, 4) if ntiles % (2 * q) == 0)
    x_col0 = 3 * wcol // rb
    y_col0 = x_col0 + heads
    grid_spec = pltpu.PrefetchScalarGridSpec(
        num_scalar_prefetch=1,
        grid=(nb, heads),
        in_specs=[
            pl.BlockSpec((None, s_len, rb), lambda n, h, s: (n, 0, x_col0 + h)),
            pl.BlockSpec((None, s_len, rb), lambda n, h, s: (n, 0, y_col0 + h)),
            pl.BlockSpec((None, 4, rb), lambda n, h, s: (s[0], 0, h)),
            pl.BlockSpec((None, 1, rb), lambda n, h, s: (s[0], 0, h)),
            pl.BlockSpec((None, None, rb, 4 * rb), lambda n, h, s: (s[0], h, 0, 0)),
            pl.BlockSpec((None, 4, rb), lambda n, h, s: (s[0], 0, h)),
            pl.BlockSpec((None, 2, rb), lambda n, h, s: (s[0], 0, h)),
        ],
        out_specs=pl.BlockSpec((None, s_len, rb), lambda n, h, s: (n, 0, h)),
        scratch_shapes=[pltpu.VMEM((s_len, rb), F32)] * 5,
    )
    return pl.pallas_call(
        functools.partial(_lru_kernel, rows_per_block=rows_per_block, nsplit=nsplit),
        grid_spec=grid_spec,
        out_shape=jax.ShapeDtypeStruct((nb, s_len, dr), BF16),
        compiler_params=_compiler_params(("parallel", "parallel")),
        name="lru",
    )(sel, proj, proj, conv_b_w, conv_b_b, half_w_gate, half_b_gate, lam)


def _mixer_out_kernel(s_ref, x_ref, c_ref, b_ref, v_ref, cp_ref, vp_ref, cn_ref, vn_ref, yb_ref,
                      *rest, alpha, ncol, blocks_per_seq):
    del s_ref
    ga_refs = rest[:ncol]
    gb_refs = rest[ncol:2 * ncol]
    cw_ref, woa_ref, wob_ref, wo_ref, bo_ref, g_ref, bb_ref, o_ref = rest[2 * ncol:]
    tm = x_ref.shape[0]
    block_in_seq = pl.program_id(0) % blocks_per_seq

    cv = c_ref[...] * v_ref[...]
    prev_tile = cp_ref[...] * vp_ref[...]
    next_tile = cn_ref[...] * vn_ref[...]
    prev_tile = jnp.where(block_in_seq == 0, _seq_start_tile(prev_tile), prev_tile)
    next_tile = jnp.where(block_in_seq == blocks_per_seq - 1, _seq_end_tile(next_tile), next_tile)
    cv_m1 = jnp.concatenate([prev_tile, cv[:tm - SUBLANES]], axis=0)
    cv_p1 = jnp.concatenate([cv[SUBLANES:], next_tile], axis=0)
    cw = cw_ref[...]
    ya_pre = b_ref[...] * (cw[0:1, :] * cv_m1 + cw[1:2, :] * cv + cw[2:3, :] * cv_p1)

    y_a = _dot(ya_pre.astype(BF16), woa_ref[...])
    y_b = _dot(yb_ref[...], wob_ref[...])
    w = ga_refs[0].shape[1]
    parts = []
    for c in range(ncol):
        sl = slice(c * w, (c + 1) * w)
        parts.append(jax.nn.sigmoid(ga_refs[c][...]) * y_a[:, sl]
                     + jax.nn.sigmoid(gb_refs[c][...]) * y_b[:, sl])
    merged = jnp.concatenate(parts, axis=1).astype(BF16)
    out = _dot(merged, wo_ref[...]) + bo_ref[...]
    o_ref[...] = _layer_norm(alpha * x_ref[...] + out, g_ref[...], bb_ref[...])


def _mixer_out_call(x, sel, yb_pre, proj_parts, conv_a_w, w_out_a, w_out_b, w_o, b_o, ln_g, ln_b, *,
                    alpha, tm, s_len):
    t, d = x.shape
    dc = conv_a_w.shape[2]
    dr = yb_pre.shape[1]
    w = dc
    ncol = d // w

    def locate(col):
        for part in proj_parts:
            nblocks = part.shape[1] // w
            if col < nblocks:
                return part, col
            col -= nblocks
        raise ValueError("column block outside the projection")

    ga_col0 = 3 + 2 * (dr // w)
    gb_col0 = ga_col0 + ncol
    assert s_len % tm == 0
    blocks_per_seq = s_len // tm
    tiles_per_block = tm // SUBLANES
    tiles_per_seq = s_len // SUBLANES
    resident = pl.Buffered(1)

    def col_spec(col):
        local = locate(col)[1]
        return pl.BlockSpec((tm, w), lambda i, s: (i, local))

    def neighbour_tile_spec(col, offset):
        local = locate(col)[1]

        def index(i, s):
            seq = i // blocks_per_seq
            tile = ((i % blocks_per_seq) * tiles_per_block + offset + tiles_per_seq) % tiles_per_seq
            return (seq * tiles_per_seq + tile, local)
        return pl.BlockSpec((SUBLANES, w), index)

    conv_cols = [0, 1, 2, 0, 2, 0, 2]
    gate_cols = [ga_col0 + c for c in range(ncol)] + [gb_col0 + c for c in range(ncol)]
    in_specs = (
        [pl.BlockSpec((tm, d), lambda i, s: (i, 0)),
         col_spec(0), col_spec(1), col_spec(2),
         neighbour_tile_spec(0, -1), neighbour_tile_spec(2, -1),
         neighbour_tile_spec(0, tiles_per_block), neighbour_tile_spec(2, tiles_per_block),
         pl.BlockSpec((tm, dr), lambda i, s: (i, 0))]
        + [col_spec(col) for col in gate_cols]
        + [pl.BlockSpec((None, 3, dc), lambda i, s: (s[0], 0, 0)),
           pl.BlockSpec((None, dc, d), lambda i, s: (s[0], 0, 0), pipeline_mode=resident),
           pl.BlockSpec((None, dr, d), lambda i, s: (s[0], 0, 0), pipeline_mode=resident),
           pl.BlockSpec((None, d, d), lambda i, s: (s[0], 0, 0), pipeline_mode=resident),
           pl.BlockSpec((None, 1, d), lambda i, s: (s[0], 0, 0)),
           pl.BlockSpec((None, None, 1, d), lambda i, s: (s[0], s[2], 0, 0)),
           pl.BlockSpec((None, None, 1, d), lambda i, s: (s[0], s[2], 0, 0))])
    grid_spec = pltpu.PrefetchScalarGridSpec(
        num_scalar_prefetch=1,
        grid=(t // tm,),
        in_specs=in_specs,
        out_specs=pl.BlockSpec((tm, d), lambda i, s: (i, 0)),
    )
    return pl.pallas_call(
        functools.partial(_mixer_out_kernel, alpha=alpha, ncol=ncol, blocks_per_seq=blocks_per_seq),
        grid_spec=grid_spec,
        out_shape=jax.ShapeDtypeStruct((t, d), F32),
        compiler_params=_compiler_params(("parallel",)),
        name="mixer_out",
    )(sel, x, *[locate(col)[0] for col in conv_cols], yb_pre,
      *[locate(col)[0] for col in gate_cols], conv_a_w, w_out_a, w_out_b, w_o, b_o, ln_g, ln_b)


def _pick_tile(n, target):
    if n <= target:
        return n
    for cand in range(target, 7, -1):
        if n % cand == 0 and cand % SUBLANES == 0:
            return cand
    return n


def _interleave_segments(x):
    nb, s_len, d = x.shape
    return x.reshape(nb, SUBLANES, s_len // SUBLANES, d).transpose(0, 2, 1, 3).reshape(nb, s_len, d)


def _deinterleave_segments(x):
    nb, s_len, d = x.shape
    return x.reshape(nb, s_len // SUBLANES, SUBLANES, d).transpose(0, 2, 1, 3).reshape(nb, s_len, d)


def _trunk(x_prompt, x_sample, ln_gain, ln_bias, ffn_w_gate_up, ffn_w_down, w_in, b_in,
           conv_a_w, w_out_a, conv_b_w, conv_b_b, lru_w_gate, lru_b_gate, lru_lambda,
           w_out_b, w_o, b_o, *, tiles=None):
    tiles = dict(tiles or {})
    depth = ln_gain.shape[0]
    alpha = (2.0 * depth) ** 0.25
    bp, s_len, d = x_prompt.shape
    bs, s_len2, _ = x_sample.shape
    assert s_len == s_len2, "prompt and sample sequences are processed as one batch of sequences"
    nb = bp + bs
    t = nb * s_len
    dc = conv_a_w.shape[2]
    dr = conv_b_w.shape[2]
    heads, rb = lru_w_gate.shape[3], lru_w_gate.shape[4]
    f = ffn_w_down.shape[2]
    n_in = w_in.shape[2]
    assert n_in == 3 * dc + 2 * dr + 2 * d and dr % dc == 0 and d % dc == 0 and dc % rb == 0
    assert heads * rb == dr and s_len % (SUBLANES * SUBLANES) == 0

    fc = tiles.get("fc", 512)
    fp = _round_up(f, fc)
    tm_ffn = tiles.get("tm_ffn", _pick_tile(t, 1024))
    tm_in = tiles.get("tm_in", _pick_tile(t, 1024))
    tm_out = tiles.get("tm_out", _pick_tile(s_len, 256))
    seq_rows = tiles.get("seq_rows", _pick_tile(s_len, 512))
    wide_chunks = n_in // (2 * dc)
    assert 2 * dc * wide_chunks >= 3 * dc + 2 * dr

    gate_w, up_w = ffn_w_gate_up[..., :f], ffn_w_gate_up[..., f:]
    pad_cols = [(0, 0)] * 3 + [(0, fp - f)]
    w_g = jnp.pad(gate_w, pad_cols).astype(BF16)
    w_u = jnp.pad(up_w, pad_cols).astype(BF16)
    w_d =jnp.pad(ffn_w_down, [(0, 0), (0, 0), (0, fp - f), (0, 0)]).astype(BF16)
    w_in_b = w_in.astype(BF16)
    half_w_gate = (0.5 * jnp.transpose(lru_w_gate, (0, 3, 4, 1, 2, 5))).reshape(
        depth, heads, rb, 4 * rb).astype(BF16)
    half_b_gate = 0.5 * lru_b_gate.reshape(depth, 4, dr)
    w_out_a_b = w_out_a.astype(BF16)
    w_out_b_b = w_out_b.astype(BF16)
    w_o_b = w_o.astype(BF16)
    ln_g = ln_gain[:, :, None, :]
    ln_b = ln_bias[:, :, None, :]
    b_in3 = b_in[:, None, :]
    conv_b_b3 = conv_b_b[:, None, :]
    b_o3 = b_o[:, None, :]

    x0 = _interleave_segments(jnp.concatenate([x_prompt, x_sample], axis=0)).reshape(t, d)

    def layer(l, x):
        def sel(k, ln):
            return jnp.stack([l, jnp.int32(k), jnp.int32(ln)]).astype(jnp.int32)

        x = _ffn_call(x, sel(0, 0), w_g, w_u, w_d, ln_g, ln_b, alpha=alpha, tm=tm_ffn, fc=fc)
        proj_parts = [_inproj_call(x, sel(0, 0), w_in_b, b_in3, tm=tm_in, tn=2 * dc,
                                   first_chunk=0, nchunks=wide_chunks)]
        if n_in // dc > 2 * wide_chunks:
            proj_parts.append(_inproj_call(x, sel(0, 0), w_in_b, b_in3, tm=tm_in, tn=dc,
                                           first_chunk=2 * wide_chunks, nchunks=1))
        yb_pre = _lru_call(proj_parts[0].reshape(nb, s_len, 2 * dc * wide_chunks), sel(0, 0),
                           conv_b_w, conv_b_b3, half_w_gate, half_b_gate, lru_lambda, wcol=dc,
                           rows_per_block=seq_rows)
        x = _mixer_out_call(x, sel(0, 1), yb_pre.reshape(t, dr), proj_parts, conv_a_w,
                            w_out_a_b, w_out_b_b, w_o_b, b_o3, ln_g, ln_b, alpha=alpha, tm=tm_out,
                            s_len=s_len)
        x = _ffn_call(x, sel(1, 2), w_g, w_u, w_d, ln_g, ln_b, alpha=alpha, tm=tm_ffn, fc=fc)
        return x

    x = lax.fori_loop(0, depth, layer, x0)
    x = x.reshape(nb, s_len, d)
    return _deinterleave_segments(x[:bp]), _deinterleave_segments(x[bp:])


def kernel(x_prompt, x_sample, ln_gain, ln_bias, ffn_w_gate_up, ffn_w_down, w_in, b_in, conv_a_w, w_out_a, conv_b_w, conv_b_b, lru_w_gate, lru_b_gate, lru_lambda, w_out_b, w_o, b_o):
    return _trunk(x_prompt, x_sample, ln_gain, ln_bias, ffn_w_gate_up, ffn_w_down, w_in, b_in,
                  conv_a_w, w_out_a, conv_b_w, conv_b_b, lru_w_gate, lru_b_gate, lru_lambda,
                  w_out_b, w_o, b_o)
```
